```python
import jax, jax.numpy as jnp
from jax import lax
import numpy as np

D_MODEL = 1024
BATCH = 16
SEQ = 2048
DEPTH = 1

D_FF = 2816
N_HEADS = 8
QK_NOPE_DIM = 64
QK_ROPE_DIM = 32
QK_HEAD_DIM = QK_NOPE_DIM + QK_ROPE_DIM
V_HEAD_DIM = 64
Q_LORA_RANK = 384
KV_LORA_RANK = 256
CONV_DIM = 1024
CONV_WIDTH = 3
ROPE_THETA = 10000.0
Q_BLOCK = 128
NORM_EPS = 1e-6
ATTN_OUT_DIM = N_HEADS * V_HEAD_DIM
N_BRANCHES = 2
IN_SIZES = (Q_LORA_RANK, KV_LORA_RANK, QK_ROPE_DIM, CONV_DIM, CONV_DIM, CONV_DIM, D_MODEL, D_MODEL)
IN_DIM = sum(IN_SIZES)
IN_SPLITS = tuple(int(s) for s in np.cumsum(IN_SIZES)[:-1])

kernel_name = "hybrid_mla_shortconv_macaron_block"


def rmsnorm(x, gain):
    x32 = x.astype(jnp.float32)
    y = x32 * lax.rsqrt(jnp.mean(x32 * x32, axis=-1, keepdims=True) + NORM_EPS)
    return (y * gain.astype(jnp.float32)).astype(x.dtype)


def swiglu(h, w_gate, w_up, w_down):
    return (jax.nn.silu(h @ w_gate) * (h @ w_up)) @ w_down


def rope(t, positions):
    half = QK_ROPE_DIM // 2
    inv_freq = 1.0 / (ROPE_THETA ** (jnp.arange(half, dtype=jnp.float32) / half))
    ang = positions.astype(jnp.float32)[..., None] * inv_freq
    ang = ang.reshape(ang.shape[:2] + (1,) * (t.ndim - 3) + (half,))
    cos, sin = jnp.cos(ang).astype(t.dtype), jnp.sin(ang).astype(t.dtype)
    t1, t2 = t[..., :half], t[..., half:]
    return jnp.concatenate([t1 * cos - t2 * sin, t1 * sin + t2 * cos], axis=-1)


def causal_block_attention(q, k, v):
    b, s, h, dq = q.shape
    nb = s // Q_BLOCK
    scale = QK_HEAD_DIM ** -0.5
    q_blocks = q.reshape(b, nb, Q_BLOCK, h, dq).transpose(1, 0, 2, 3, 4)
    key_pos = jnp.arange(s)

    def one_block(args):
        qb, blk = args
        scores = jnp.einsum('bqhd,bkhd->bhqk', qb, k).astype(jnp.float32) * scale
        q_pos = blk * Q_BLOCK + jnp.arange(Q_BLOCK)
        mask = key_pos[None, :] <= q_pos[:, None]
        scores = jnp.where(mask[None, None], scores, -1e30)
        p = jax.nn.softmax(scores, axis=-1).astype(v.dtype)
        return jnp.einsum('bhqk,bkhd->bqhd', p, v)

    out = lax.map(one_block, (q_blocks, jnp.arange(nb)))
    return out.transpose(1, 0, 2, 3, 4).reshape(b, s, h * V_HEAD_DIM)


def setup_inputs(seed: int = 0) -> dict:
    key = jax.random.key(seed)
    ks = jax.random.split(key, 32)

    def w(k, shape, fan_in):
        return jax.random.normal(k, shape, jnp.float32) * fan_in ** -0.5

    def gain(k, n):
        return 1.0 + 0.02 * jax.random.normal(k, (n,), jnp.float32)

    positions = jnp.broadcast_to(jnp.arange(SEQ, dtype=jnp.int32)[None, :], (BATCH, SEQ))
    return {
        "x": jax.random.normal(ks[0], (BATCH, SEQ, D_MODEL), jnp.float32),
        "positions": positions,
        "ffn1_norm": gain(ks[1], D_MODEL),
        "ffn1_w_gate": w(ks[2], (D_MODEL, D_FF), D_MODEL),
        "ffn1_w_up": w(ks[3], (D_MODEL, D_FF), D_MODEL),
        "ffn1_w_down": w(ks[4], (D_FF, D_MODEL), D_FF),
        "mix_norm": gain(ks[5], D_MODEL),
        "w_in": w(ks[6], (D_MODEL, IN_DIM), D_MODEL),
        "gate_bias": 0.01 * jax.random.normal(ks[7], (N_BRANCHES * D_MODEL,), jnp.float32),
        "q_a_norm": gain(ks[8], Q_LORA_RANK),
        "w_uq": w(ks[9], (Q_LORA_RANK, N_HEADS * QK_HEAD_DIM), Q_LORA_RANK),
        "kv_a_norm": gain(ks[10], KV_LORA_RANK),
        "w_uk": w(ks[11], (KV_LORA_RANK, N_HEADS * QK_NOPE_DIM), KV_LORA_RANK),
        "w_uv": w(ks[12], (KV_LORA_RANK, N_HEADS * V_HEAD_DIM), KV_LORA_RANK),
        "q_head_norm": gain(ks[13], QK_HEAD_DIM),
        "k_head_norm": gain(ks[14], QK_HEAD_DIM),
        "w_proj_attn": w(ks[15], (ATTN_OUT_DIM, D_MODEL), ATTN_OUT_DIM),
        "conv_w": w(ks[16], (CONV_WIDTH, CONV_DIM), CONV_WIDTH),
        "w_proj_conv": w(ks[17], (CONV_DIM, D_MODEL), CONV_DIM),
        "w_out": w(ks[18], (D_MODEL, D_MODEL), D_MODEL),
        "ffn2_norm": gain(ks[19], D_MODEL),
        "ffn2_w_gate": w(ks[20], (D_MODEL, D_FF), D_MODEL),
        "ffn2_w_up": w(ks[21], (D_MODEL, D_FF), D_MODEL),
        "ffn2_w_down": w(ks[22], (D_FF, D_MODEL), D_FF),
    }


def reference(x, positions, ffn1_norm, ffn1_w_gate, ffn1_w_up, ffn1_w_down,
              mix_norm, w_in, gate_bias, q_a_norm, w_uq, kv_a_norm, w_uk, w_uv,
              q_head_norm, k_head_norm, w_proj_attn, conv_w, w_proj_conv, w_out,
              ffn2_norm, ffn2_w_gate, ffn2_w_up, ffn2_w_down):
    b, s, _ = x.shape
    for _layer in range(DEPTH):
        x = x + 0.5 * swiglu(rmsnorm(x, ffn1_norm), ffn1_w_gate, ffn1_w_up, ffn1_w_down)

        h = rmsnorm(x, mix_norm)
        proj = h @ w_in
        q_lat, kv_lat, k_rope_raw, xc, gB, gC, gate_logits = jnp.split(proj, IN_SPLITS, axis=-1)[:7] + []  if False else jnp.split(proj, IN_SPLITS, axis=-1)[:7]
        gate_logits = proj[..., IN_SPLITS[-1]:] if False else jnp.concatenate([gate_logits, proj[..., IN_SPLITS[-1]:]], axis=-1)

        q = (rmsnorm(q_lat, q_a_norm) @ w_uq).reshape(b, s, N_HEADS, QK_HEAD_DIM)
        c_kv = rmsnorm(kv_lat, kv_a_norm)
        k_nope = (c_kv @ w_uk).reshape(b, s, N_HEADS, QK_NOPE_DIM)
        v = (c_kv @ w_uv).reshape(b, s, N_HEADS, V_HEAD_DIM)
        k_rope = jnp.broadcast_to(k_rope_raw[:, :, None, :], (b, s, N_HEADS, QK_ROPE_DIM))
        k = jnp.concatenate([k_nope, k_rope], axis=-1)
        q = rmsnorm(q, q_head_norm)
        k = rmsnorm(k, k_head_norm)
        q = jnp.concatenate([q[..., :QK_NOPE_DIM], rope(q[..., QK_NOPE_DIM:], positions)], axis=-1)
        k = jnp.concatenate([k[..., :QK_NOPE_DIM], rope(k[..., QK_NOPE_DIM:], positions)], axis=-1)
        y_a = causal_block_attention(q, k, v) @ w_proj_attn

        u = gC * xc
        up = jnp.pad(u, ((0, 0), (CONV_WIDTH - 1, 0), (0, 0)))
        z = conv_w[0] * up[:, :s] + conv_w[1] * up[:, 1:s + 1] + conv_w[2] * up[:, 2:s + 2]
        y_b = (gB * z) @ w_proj_conv

        gates = jax.nn.sigmoid(gate_logits + gate_bias)
        merged = gates[..., :D_MODEL] * y_a + gates[..., D_MODEL:] * y_b
        x = x + merged @ w_out

        x = x + 0.5 * swiglu(rmsnorm(x, ffn2_norm), ffn2_w_gate, ffn2_w_up, ffn2_w_down)
    return x
```

```python
import functools

import jax
import jax.numpy as jnp
from jax import lax
from jax.experimental import pallas as pl
from jax.experimental.pallas import tpu as pltpu

D_MODEL = 1024
D_FF = 2816
N_HEADS = 8
QK_NOPE_DIM = 64
QK_ROPE_DIM = 32
QK_HEAD_DIM = QK_NOPE_DIM + QK_ROPE_DIM
V_HEAD_DIM = 64
Q_LORA_RANK = 384
KV_LORA_RANK = 256
CONV_DIM = 1024
CONV_WIDTH = 3
ROPE_THETA = 10000.0
NORM_EPS = 1e-6

LANES = 128
SUBLANES = 8
HEAD_PAD = LANES
ROPE_HALF = QK_ROPE_DIM // 2
ROPE_LO = QK_NOPE_DIM
ROPE_MID = QK_NOPE_DIM + ROPE_HALF
ROPE_HI = QK_HEAD_DIM

OFF_Q = 0
OFF_KV = OFF_Q + Q_LORA_RANK
OFF_KR = OFF_KV + KV_LORA_RANK
OFF_XC = OFF_KR + HEAD_PAD
OFF_GB = OFF_XC + CONV_DIM
OFF_GC = OFF_GB + CONV_DIM
OFF_GA_LOGIT = OFF_GC + CONV_DIM
OFF_GB_LOGIT = OFF_GA_LOGIT + D_MODEL
IN_DIM_PAD = OFF_GB_LOGIT + D_MODEL

FF_CHUNKS = (768, 768, 768, 512)
VMEM_LIMIT = 56 * 1024 * 1024

TM_FFN = 512
TM_MIX = 512
TQ = 256
TK = 256
MASK_VALUE = -1e30


def _bf16(x):
    return x.astype(jnp.bfloat16)


def _dot(a, b):
    return jnp.dot(a, b, preferred_element_type=jnp.float32)


def _rmsnorm(x, gain, n):
    ms = jnp.sum(x * x, axis=-1, keepdims=True) * (1.0 / n)
    return x * lax.rsqrt(ms + NORM_EPS) * gain


def _swiglu_half_step(x, gain, wg_ref, wu_ref, wd_ref, act_ref):
    h = _bf16(_rmsnorm(x, gain, D_MODEL))
    c0 = 0
    for ck in FF_CHUNKS:
        g = _dot(h, wg_ref[:, c0:c0 + ck])
        u = _dot(h, wu_ref[:, c0:c0 + ck])
        act_ref[:, c0:c0 + ck] = _bf16(g * jax.nn.sigmoid(g) * u)
        c0 += ck
    return x + 0.5 * _dot(act_ref[...], wd_ref[...])


def _ffn_kernel(x_ref, gain_ref, wg_ref, wu_ref, wd_ref, o_ref, act_ref):
    o_ref[...] = _swiglu_half_step(x_ref[...], gain_ref[...], wg_ref, wu_ref, wd_ref, act_ref)


def _rope_tables(pos, inv_freq):
    ang = pos * inv_freq
    lane = lax.broadcasted_iota(jnp.int32, ang.shape, 1)
    is_rope = (lane >= ROPE_LO) & (lane < ROPE_HI)
    cos_t = jnp.where(is_rope, jnp.cos(ang), 1.0)
    sin = jnp.sin(ang)
    sin_t = jnp.where(is_rope, jnp.where(lane < ROPE_MID, -sin, sin), 0.0)
    return cos_t, sin_t, lane < ROPE_MID


def _head_norm_rope(t, gain, cos_t, sin_t, first_half):
    t = _rmsnorm(t, gain, QK_HEAD_DIM)
    partner = jnp.where(first_half, pltpu.roll(t, LANES - ROPE_HALF, 1), pltpu.roll(t, ROPE_HALF, 1))
    return t * cos_t + partner * sin_t


def _mix_kernel(seq_len, x_ref, pos_ref, invf_ref, mixg_ref, win_ref, bias_ref, qag_ref, wuq_ref,
                kvag_ref, wuk_ref, wuv_ref, qhg_ref, khg_ref, convw_ref, wpc_ref,
                q_ref, k_ref, v_ref, ga_ref, gyb_ref, ubuf_ref):
    tm = x_ref.shape[0]
    h = _bf16(_rmsnorm(x_ref[...], mixg_ref[...], D_MODEL))

    def proj(off, width):
        return _dot(h, win_ref[:, off:off + width])

    cos_t, sin_t, first_half = _rope_tables(pos_ref[...], invf_ref[...])

    qn = _bf16(_rmsnorm(proj(OFF_Q, Q_LORA_RANK), qag_ref[...], Q_LORA_RANK))
    q = _dot(qn, wuq_ref[...])
    ckv = _bf16(_rmsnorm(proj(OFF_KV, KV_LORA_RANK), kvag_ref[...], KV_LORA_RANK))
    k_nope = _dot(ckv, wuk_ref[...])
    k_rope = proj(OFF_KR, HEAD_PAD)
    v_ref[...] = _bf16(_dot(ckv, wuv_ref[...]))
    q_gain = qhg_ref[...] * (QK_HEAD_DIM ** -0.5)
    for hd in range(N_HEADS):
        sl = slice(hd * HEAD_PAD, (hd + 1) * HEAD_PAD)
        q_ref[:, sl] = _bf16(_head_norm_rope(q[:, sl], q_gain, cos_t, sin_t, first_half))
        k_ref[:, sl] = _bf16(_head_norm_rope(k_nope[:, sl] + k_rope, khg_ref[...], cos_t, sin_t, first_half))

    @pl.when((pl.program_id(0) * tm) % seq_len == 0)
    def _():
        ubuf_ref[0:SUBLANES, :] = jnp.zeros((SUBLANES, CONV_DIM), jnp.float32)

    u = proj(OFF_GC, CONV_DIM) * proj(OFF_XC, CONV_DIM)
    ubuf_ref[SUBLANES:SUBLANES + tm, :] = u
    z = (convw_ref[0:1, :] * ubuf_ref[SUBLANES - 2:SUBLANES - 2 + tm, :]
         + convw_ref[1:2, :] * ubuf_ref[SUBLANES - 1:SUBLANES - 1 + tm, :]
         + convw_ref[2:3, :] * u)
    ubuf_ref[0:SUBLANES, :] = ubuf_ref[tm:tm + SUBLANES, :]
    y_b = _dot(_bf16(proj(OFF_GB, CONV_DIM) * z), wpc_ref[...])

    ga_ref[...] = _bf16(jax.nn.sigmoid(proj(OFF_GA_LOGIT, D_MODEL) + bias_ref[:, 0:D_MODEL]))
    gate_b = jax.nn.sigmoid(proj(OFF_GB_LOGIT, D_MODEL) + bias_ref[:, D_MODEL:2 * D_MODEL])
    gyb_ref[...] = _bf16(gate_b * y_b)


def _attn_kernel(q_ref, k_ref, v_ref, o_ref):
    qi = pl.program_id(2)
    tq = q_ref.shape[0]
    lane = lax.broadcasted_iota(jnp.int32, (tq, LANES), 1)
    row = lax.broadcasted_iota(jnp.int32, (tq, TK), 0)
    col = lax.broadcasted_iota(jnp.int32, (tq, TK), 1)
    outs = []
    for hd in range(2):
        q = q_ref[:, hd * HEAD_PAD:(hd + 1) * HEAD_PAD]

        def step(j, carry, masked, q=q, hd=hd):
            m, l, acc = carry
            ks = pl.multiple_of(j * TK, TK)
            k = k_ref[pl.ds(ks, TK), hd * HEAD_PAD:(hd + 1) * HEAD_PAD]
            s = lax.dot_general(q, k, (((1,), (1,)), ((), ())), preferred_element_type=jnp.float32)
            if masked:
                s = jnp.where(col <= row, s, MASK_VALUE)
            m_new = jnp.maximum(m, jnp.max(s, axis=-1, keepdims=True))
            alpha = jnp.exp(m - m_new)
            p = jnp.exp(s - m_new)
            l = alpha * l + jnp.sum(p, axis=-1, keepdims=True)
            acc = alpha * acc + _dot(_bf16(p), v_ref[pl.ds(ks, TK), :])
            return m_new, l, acc

        init = (jnp.full((tq, 1), MASK_VALUE, jnp.float32), jnp.zeros((tq, 1), jnp.float32),
                jnp.zeros((tq, LANES), jnp.float32))
        carry = lax.fori_loop(0, qi, functools.partial(step, masked=False), init)
        _, l, acc = step(qi, carry, masked=True)
        outs.append(acc / l)
    o_ref[...] = _bf16(jnp.where(lane < V_HEAD_DIM, outs[0], outs[1]))


def _post_kernel(x_ref, attn_ref, ga_ref, gyb_ref, wpa_ref, wout_ref, gain_ref, wg_ref, wu_ref, wd_ref,
                 o_ref, act_ref):
    y_a = _dot(attn_ref[...], wpa_ref[...])
    merged = ga_ref[...].astype(jnp.float32) * y_a + gyb_ref[...].astype(jnp.float32)
    x2 = x_ref[...] + _dot(_bf16(merged), wout_ref[...])
    o_ref[...] = _swiglu_half_step(x2, gain_ref[...], wg_ref, wu_ref, wd_ref, act_ref)


def _rows(tm, width):
    return pl.BlockSpec((tm, width), lambda i: (i, 0))


def _resident():
    return pl.BlockSpec(memory_space=pltpu.VMEM)


def _params(n_axes):
    return pltpu.CompilerParams(dimension_semantics=("arbitrary",) * n_axes, vmem_limit_bytes=VMEM_LIMIT)


def _pad_heads(w, real, pad):
    kdim = w.shape[0]
    w = w.reshape(kdim, N_HEADS, real)
    return jnp.pad(w, ((0, 0), (0, 0), (0, pad - real))).reshape(kdim, N_HEADS * pad)


def kernel(x, positions, ffn1_norm, ffn1_w_gate, ffn1_w_up, ffn1_w_down, mix_norm, w_in, gate_bias, q_a_norm, w_uq, kv_a_norm, w_uk, w_uv, q_head_norm, k_head_norm, w_proj_attn, conv_w, w_proj_conv, w_out, ffn2_norm, ffn2_w_gate, ffn2_w_up, ffn2_w_down):
    b, s, d = x.shape
    t = b * s
    assert d == D_MODEL and t % TM_FFN == 0 and s % TM_MIX == 0 and s % TQ == 0 and TQ == TK
    f32 = jnp.float32
    row = lambda g: g.reshape(1, -1).astype(f32)

    x_flat = x.reshape(t, d)
    pos = positions.reshape(t, 1).astype(f32)

    sp = (Q_LORA_RANK, Q_LORA_RANK + KV_LORA_RANK, Q_LORA_RANK + KV_LORA_RANK + QK_ROPE_DIM)
    w_kr = jnp.pad(w_in[:, sp[1]:sp[2]], ((0, 0), (ROPE_LO, HEAD_PAD - ROPE_HI)))
    w_in_pad = _bf16(jnp.concatenate([w_in[:, :sp[1]], w_kr, w_in[:, sp[2]:]], axis=1))
    assert w_in_pad.shape[1] == IN_DIM_PAD
    w_uq_pad = _bf16(_pad_heads(w_uq, QK_HEAD_DIM, HEAD_PAD))
    w_uk_pad = _bf16(_pad_heads(w_uk, QK_NOPE_DIM, HEAD_PAD))
    pad_gain = lambda g: jnp.pad(g.astype(f32), (0, HEAD_PAD - QK_HEAD_DIM)).reshape(1, HEAD_PAD)
    inv_freq = 1.0 / (ROPE_THETA ** (jnp.arange(ROPE_HALF, dtype=f32) / ROPE_HALF))
    inv_freq_pad = jnp.pad(jnp.concatenate([inv_freq, inv_freq]), (ROPE_LO, HEAD_PAD - ROPE_HI)).reshape(1, HEAD_PAD)

    def ffn_weights(wg, wu, wd):
        return _bf16(wg), _bf16(wu), _bf16(wd)

    x1 = pl.pallas_call(
        _ffn_kernel,
        grid=(t // TM_FFN,),
        in_specs=[_rows(TM_FFN, d)] + [_resident()] * 4,
        out_specs=_rows(TM_FFN, d),
        out_shape=jax.ShapeDtypeStruct((t, d), f32),
        scratch_shapes=[pltpu.VMEM((TM_FFN, D_FF), jnp.bfloat16)],
        compiler_params=_params(1),
        name="ffn1",
    )(x_flat,row(ffn1_norm), *ffn_weights(ffn1_w_gate, ffn1_w_up, ffn1_w_down))

    bf = jnp.bfloat16
    q, k, v, gate_a, gated_yb = pl.pallas_call(
        functools.partial(_mix_kernel, s),
        grid=(t // TM_MIX,),
        in_specs=[_rows(TM_MIX, d), _rows(TM_MIX, 1)] + [_resident()] * 13,
        out_specs=[_rows(TM_MIX, N_HEADS * HEAD_PAD), _rows(TM_MIX, N_HEADS * HEAD_PAD),
                   _rows(TM_MIX, N_HEADS * V_HEAD_DIM), _rows(TM_MIX, d), _rows(TM_MIX, d)],
        out_shape=[jax.ShapeDtypeStruct((t, N_HEADS * HEAD_PAD), bf), jax.ShapeDtypeStruct((t, N_HEADS * HEAD_PAD), bf),
                   jax.ShapeDtypeStruct((t, N_HEADS * V_HEAD_DIM), bf), jax.ShapeDtypeStruct((t, d), bf),
                   jax.ShapeDtypeStruct((t, d), bf)],
        scratch_shapes=[pltpu.VMEM((TM_MIX + 2 * SUBLANES, CONV_DIM), f32)],
        compiler_params=_params(1),
        name="mix",
    )(x1, pos, inv_freq_pad, row(mix_norm), w_in_pad, row(gate_bias), row(q_a_norm), w_uq_pad,
      row(kv_a_norm), w_uk_pad, _bf16(w_uv), pad_gain(q_head_norm), pad_gain(k_head_norm),
      conv_w.astype(f32), _bf16(w_proj_conv))

    pair = 2 * HEAD_PAD
    attn = pl.pallas_call(
        _attn_kernel,
        grid=(b, N_HEADS // 2, s // TQ),
        in_specs=[pl.BlockSpec((None, TQ, pair), lambda bi, hp, qi: (bi, qi, hp)),
                  pl.BlockSpec((None, s, pair), lambda bi, hp, qi: (bi, 0, hp)),
                  pl.BlockSpec((None, s, LANES), lambda bi, hp, qi: (bi, 0, hp))],
        out_specs=pl.BlockSpec((None, TQ, LANES), lambda bi, hp, qi: (bi, qi, hp)),
        out_shape=jax.ShapeDtypeStruct((b, s, N_HEADS * V_HEAD_DIM), bf),
        compiler_params=_params(3),
        name="attn",
    )(q.reshape(b, s, -1), k.reshape(b, s, -1), v.reshape(b, s, -1))

    out = pl.pallas_call(
        _post_kernel,
        grid=(t // TM_FFN,),
        in_specs=[_rows(TM_FFN, d), _rows(TM_FFN, N_HEADS * V_HEAD_DIM), _rows(TM_FFN, d), _rows(TM_FFN, d)]
        + [_resident()] * 6,
        out_specs=_rows(TM_FFN, d),
        out_shape=jax.ShapeDtypeStruct((t, d), f32),
        scratch_shapes=[pltpu.VMEM((TM_FFN, D_FF), jnp.bfloat16)],
        compiler_params=_params(1),
        name="post",
    )(x1, attn.reshape(t, -1), gate_a, gated_yb, _bf16(w_proj_attn), _bf16(w_out), row(ffn2_norm),
      *ffn_weights(ffn2_w_gate, ffn2_w_up, ffn2_w_down))
    return out.reshape(b, s, d)
```

```python
import functools

import jax
import jax.numpy as jnp
from jax import lax
from jax.experimental import pallas as pl
from jax.experimental.pallas import tpu as pltpu

D_MODEL = 1024
D_FF = 2816
N_HEADS = 8
QK_NOPE_DIM = 64
QK_ROPE_DIM = 32
QK_HEAD_DIM = QK_NOPE_DIM + QK_ROPE_DIM
V_HEAD_DIM = 64
Q_LORA_RANK = 384
KV_LORA_RANK = 256
CONV_DIM = 1024
CONV_WIDTH = 3
ROPE_THETA = 10000.0
NORM_EPS = 1e-6

LANES = 128
SUBLANES = 8
HEAD_PAD = LANES
ROPE_HALF = QK_ROPE_DIM // 2
ROPE_LO = QK_NOPE_DIM
ROPE_MID = QK_NOPE_DIM + ROPE_HALF
ROPE_HI = QK_HEAD_DIM

OFF_Q = 0
OFF_KV = OFF_Q + Q_LORA_RANK
OFF_KR = OFF_KV + KV_LORA_RANK
OFF_XC = OFF_KR + HEAD_PAD
OFF_GB = OFF_XC + CONV_DIM
OFF_GC = OFF_GB + CONV_DIM
OFF_GA_LOGIT = OFF_GC + CONV_DIM
OFF_GB_LOGIT = OFF_GA_LOGIT + D_MODEL
IN_DIM_PAD = OFF_GB_LOGIT + D_MODEL

FF_CHUNKS = (768, 768, 768, 512)
VMEM_LIMIT = 56 * 1024 * 1024

TM_FFN = 512
TM_MIX = 512
TQ = 256
MASK_VALUE = -1e30


def _bf16(x):
    return x.astype(jnp.bfloat16)


def _dot(a, b):
    return jnp.dot(a, b, preferred_element_type=jnp.float32)


def _rmsnorm(x, gain, n):
    ms = jnp.sum(x * x, axis=-1, keepdims=True) * (1.0 / n)
    return x * lax.rsqrt(ms + NORM_EPS) * gain


def _swiglu_half_step(x, gain, wg_ref, wu_ref, wd_ref, act_ref):
    h = _bf16(_rmsnorm(x, gain, D_MODEL))
    c0 = 0
    for ck in FF_CHUNKS:
        g = _dot(h, wg_ref[:, c0:c0 + ck])
        u = _dot(h, wu_ref[:, c0:c0 + ck])
        act_ref[:, c0:c0 + ck] = _bf16(g * jax.nn.sigmoid(g) * u)
        c0 += ck
    return x + 0.5 * _dot(act_ref[...], wd_ref[...])


def _ffn_kernel(x_ref, gain_ref, wg_ref, wu_ref, wd_ref, o_ref, act_ref):
    o_ref[...] = _swiglu_half_step(x_ref[...], gain_ref[...], wg_ref, wu_ref, wd_ref, act_ref)


def _rope_tables(pos, inv_freq):
    ang = pos * inv_freq
    lane = lax.broadcasted_iota(jnp.int32, ang.shape, 1)
    is_rope = (lane >= ROPE_LO) & (lane < ROPE_HI)
    cos_t = jnp.where(is_rope, jnp.cos(ang), 1.0)
    sin = jnp.sin(ang)
    sin_t = jnp.where(is_rope, jnp.where(lane < ROPE_MID, -sin, sin), 0.0)
    return cos_t, sin_t, lane < ROPE_MID


def _head_norm_rope(t, gain, cos_t, sin_t, first_half):
    t = _rmsnorm(t, gain, QK_HEAD_DIM)
    partner = jnp.where(first_half, pltpu.roll(t, LANES - ROPE_HALF, 1), pltpu.roll(t, ROPE_HALF, 1))
    return t * cos_t + partner * sin_t


def _mix_kernel(seq_len, x_ref, pos_ref, invf_ref, mixg_ref, win_ref, bias_ref, qag_ref, wuq_ref,
                kvag_ref, wuk_ref, wuv_ref, qhg_ref, khg_ref, convw_ref, wpc_ref,
                q_ref, k_ref, v_ref, ga_ref, gyb_ref, ubuf_ref):
    tm = x_ref.shape[0]
    h = _bf16(_rmsnorm(x_ref[...], mixg_ref[...], D_MODEL))

    def proj(off, width):
        return _dot(h, win_ref[:, off:off + width])

    cos_t, sin_t, first_half = _rope_tables(pos_ref[...], invf_ref[...])

    qn = _bf16(_rmsnorm(proj(OFF_Q, Q_LORA_RANK), qag_ref[...], Q_LORA_RANK))
    q = _dot(qn, wuq_ref[...])
    ckv = _bf16(_rmsnorm(proj(OFF_KV, KV_LORA_RANK), kvag_ref[...], KV_LORA_RANK))
    k_nope = _dot(ckv, wuk_ref[...])
    k_rope = proj(OFF_KR, HEAD_PAD)
    v_ref[...] = _bf16(_dot(ckv, wuv_ref[...]))
    q_gain = qhg_ref[...] * (QK_HEAD_DIM ** -0.5)
    for hd in range(N_HEADS):
        sl = slice(hd * HEAD_PAD, (hd + 1) * HEAD_PAD)
        q_ref[:, sl] = _bf16(_head_norm_rope(q[:, sl], q_gain, cos_t, sin_t, first_half))
        k_ref[:, sl] = _bf16(_head_norm_rope(k_nope[:, sl] + k_rope, khg_ref[...], cos_t, sin_t, first_half))

    @pl.when((pl.program_id(0) * tm) % seq_len == 0)
    def _():
        ubuf_ref[0:SUBLANES, :] = jnp.zeros((SUBLANES, CONV_DIM), jnp.float32)

    u = proj(OFF_GC, CONV_DIM) * proj(OFF_XC, CONV_DIM)
    ubuf_ref[SUBLANES:SUBLANES + tm, :] = u
    z = (convw_ref[0:1, :] * ubuf_ref[SUBLANES - 2:SUBLANES - 2 + tm, :]
         + convw_ref[1:2, :] * ubuf_ref[SUBLANES - 1:SUBLANES - 1 + tm, :]
         + convw_ref[2:3, :] * u)
    ubuf_ref[0:SUBLANES, :] = ubuf_ref[tm:tm + SUBLANES, :]
    y_b = _dot(_bf16(proj(OFF_GB, CONV_DIM) * z), wpc_ref[...])

    ga_ref[...] = _bf16(jax.nn.sigmoid(proj(OFF_GA_LOGIT, D_MODEL) + bias_ref[:, 0:D_MODEL]))
    gate_b = jax.nn.sigmoid(proj(OFF_GB_LOGIT, D_MODEL) + bias_ref[:, D_MODEL:2 * D_MODEL])
    gyb_ref[...] = _bf16(gate_b * y_b)


def _qk(q, k):
    return lax.dot_general(q, k, (((1,), (1,)), ((), ())), preferred_element_type=jnp.float32)


def _attn_kernel(q_ref, k_ref, v_ref, o_ref):
    s_len = q_ref.shape[0]
    lane = lax.broadcasted_iota(jnp.int32, (TQ, LANES), 1)
    row = lax.broadcasted_iota(jnp.int32, (TQ, TQ), 0)
    col = lax.broadcasted_iota(jnp.int32, (TQ, TQ), 1)
    for qi in range(s_len // TQ):
        lo, hi = qi * TQ, (qi + 1) * TQ
        outs = []
        for hd in range(2):
            hs = slice(hd * HEAD_PAD, (hd + 1) * HEAD_PAD)
            q = q_ref[lo:hi, hs]
            s_diag = jnp.where(col <= row, _qk(q, k_ref[lo:hi, hs]), MASK_VALUE)
            m = jnp.max(s_diag, axis=-1, keepdims=True)
            if qi > 0:
                s_past = _qk(q, k_ref[0:lo, hs])
                m = jnp.maximum(m, jnp.max(s_past, axis=-1, keepdims=True))
            p_diag = jnp.exp(s_diag - m)
            l = jnp.sum(p_diag, axis=-1, keepdims=True)
            acc = _dot(_bf16(p_diag), v_ref[lo:hi, :])
            if qi > 0:
                p_past = jnp.exp(s_past - m)
                l = l + jnp.sum(p_past, axis=-1, keepdims=True)
                acc = acc + _dot(_bf16(p_past), v_ref[0:lo, :])
            outs.append(acc / l)
        o_ref[lo:hi, :] = _bf16(jnp.where(lane < V_HEAD_DIM, outs[0], outs[1]))


def _post_kernel(x_ref, attn_ref, ga_ref, gyb_ref, wpa_ref, wout_ref, gain_ref, wg_ref, wu_ref, wd_ref,
                 o_ref, act_ref):
    y_a = _dot(attn_ref[...], wpa_ref[...])
    merged = ga_ref[...].astype(jnp.float32) * y_a + gyb_ref[...].astype(jnp.float32)
    x2 = x_ref[...] + _dot(_bf16(merged), wout_ref[...])
    o_ref[...] = _swiglu_half_step(x2, gain_ref[...], wg_ref, wu_ref, wd_ref, act_ref)


def _rows(tm, width):
    return pl.BlockSpec((tm, width), lambda i: (i, 0))


def _resident():
    return pl.BlockSpec(memory_space=pltpu.VMEM)


def _params(n_axes):
    return pltpu.CompilerParams(dimension_semantics=("arbitrary",) * n_axes, vmem_limit_bytes=VMEM_LIMIT)


def _pad_heads(w, real, pad):
    kdim = w.shape[0]
    w = w.reshape(kdim, N_HEADS, real)
    return jnp.pad(w, ((0, 0), (0, 0), (0, pad - real))).reshape(kdim, N_HEADS * pad)


def kernel(x, positions, ffn1_norm, ffn1_w_gate, ffn1_w_up, ffn1_w_down, mix_norm, w_in, gate_bias, q_a_norm, w_uq, kv_a_norm, w_uk, w_uv, q_head_norm, k_head_norm, w_proj_attn, conv_w, w_proj_conv, w_out, ffn2_norm, ffn2_w_gate, ffn2_w_up, ffn2_w_down):
    b, s, d = x.shape
    t = b * s
    assert d == D_MODEL and t % TM_FFN == 0 and s % TM_MIX == 0 and s % TQ == 0
    f32 = jnp.float32
    row = lambda g: g.reshape(1, -1).astype(f32)

    x_flat = x.reshape(t, d)
    pos = positions.reshape(t, 1).astype(f32)

    sp = (Q_LORA_RANK, Q_LORA_RANK + KV_LORA_RANK, Q_LORA_RANK + KV_LORA_RANK + QK_ROPE_DIM)
    w_kr = jnp.pad(w_in[:, sp[1]:sp[2]], ((0, 0), (ROPE_LO, HEAD_PAD - ROPE_HI)))
    w_in_pad = _bf16(jnp.concatenate([w_in[:, :sp[1]], w_kr, w_in[:, sp[2]:]], axis=1))
    assert w_in_pad.shape[1] == IN_DIM_PAD
    w_uq_pad = _bf16(_pad_heads(w_uq, QK_HEAD_DIM, HEAD_PAD))
    w_uk_pad = _bf16(_pad_heads(w_uk, QK_NOPE_DIM, HEAD_PAD))
    pad_gain = lambda g: jnp.pad(g.astype(f32), (0, HEAD_PAD - QK_HEAD_DIM)).reshape(1, HEAD_PAD)
    inv_freq = 1.0 / (ROPE_THETA ** (jnp.arange(ROPE_HALF, dtype=f32) / ROPE_HALF))
    inv_freq_pad = jnp.pad(jnp.concatenate([inv_freq, inv_freq]), (ROPE_LO, HEAD_PAD - ROPE_HI)).reshape(1, HEAD_PAD)

    def ffn_weights(wg, wu, wd):
        return _bf16(wg), _bf16(wu), _bf16(wd)

    x1 = pl.pallas_call(
        _ffn_kernel,
        grid=(t // TM_FFN,),
        in_specs=[_rows(TM_FFN, d)] + [_resident()] * 4,
        out_specs=_rows(TM_FFN, d),
        out_shape=jax.ShapeDtypeStruct((t, d), f32),
        scratch_shapes=[pltpu.VMEM((TM_FFN, D_FF), jnp.bfloat16)],
        compiler_params=_params(1),
        name="ffn1",
    )(x_flat,row(ffn1_norm), *ffn_weights(ffn1_w_gate, ffn1_w_up, ffn1_w_down))

    bf = jnp.bfloat16
    q, k, v, gate_a, gated_yb = pl.pallas_call(
        functools.partial(_mix_kernel, s),
        grid=(t // TM_MIX,),
        in_specs=[_rows(TM_MIX, d), _rows(TM_MIX, 1)] + [_resident()] * 13,
        out_specs=[_rows(TM_MIX, N_HEADS * HEAD_PAD), _rows(TM_MIX, N_HEADS * HEAD_PAD),
                   _rows(TM_MIX, N_HEADS * V_HEAD_DIM), _rows(TM_MIX, d), _rows(TM_MIX, d)],
        out_shape=[jax.ShapeDtypeStruct((t, N_HEADS * HEAD_PAD), bf), jax.ShapeDtypeStruct((t, N_HEADS * HEAD_PAD), bf),
                   jax.ShapeDtypeStruct((t, N_HEADS * V_HEAD_DIM), bf), jax.ShapeDtypeStruct((t, d), bf),
                   jax.ShapeDtypeStruct((t, d), bf)],
        scratch_shapes=[pltpu.VMEM((TM_MIX + 2 * SUBLANES, CONV_DIM), f32)],
        compiler_params=_params(1),
        name="mix",
    )(x1, pos, inv_freq_pad, row(mix_norm), w_in_pad, row(gate_bias), row(q_a_norm), w_uq_pad,
      row(kv_a_norm), w_uk_pad, _bf16(w_uv), pad_gain(q_head_norm), pad_gain(k_head_norm),
      conv_w.astype(f32), _bf16(w_proj_conv))

    pair = 2 * HEAD_PAD
    attn = pl.pallas_call(
        _attn_kernel,
        grid=(b, N_HEADS // 2),
        in_specs=[pl.BlockSpec((None, s, pair), lambda bi, hp: (bi, 0, hp)),
                  pl.BlockSpec((None, s, pair), lambda bi, hp: (bi, 0, hp)),
                  pl.BlockSpec((None, s, LANES), lambda bi, hp: (bi, 0, hp))],
        out_specs=pl.BlockSpec((None, s, LANES), lambda bi, hp: (bi, 0, hp)),
        out_shape=jax.ShapeDtypeStruct((b, s, N_HEADS * V_HEAD_DIM), bf),
        compiler_params=_params(2),
        name="attn",
    )(q.reshape(b, s, -1), k.reshape(b, s, -1), v.reshape(b, s, -1))

    out = pl.pallas_call(
        _post_kernel,
        grid=(t // TM_FFN,),
        in_specs=[_rows(TM_FFN, d), _rows(TM_FFN, N_HEADS * V_HEAD_DIM), _rows(TM_FFN, d), _rows(TM_FFN, d)]
        + [_resident()] * 6,
        out_specs=_rows(TM_FFN, d),
        out_shape=jax.ShapeDtypeStruct((t, d), f32),
        scratch_shapes=[pltpu.VMEM((TM_FFN, D_FF), jnp.bfloat16)],
        compiler_params=_params(1),
        name="post",
    )(x1, attn.reshape(t, -1), gate_a, gated_yb, _bf16(w_proj_attn), _bf16(w_out), row(ffn2_norm),
      *ffn_weights(ffn2_w_gate, ffn2_w_up, ffn2_w_down))
    return out.reshape(b, s, d)
```

```python
import functools

import jax
import jax.numpy as jnp
import numpy as np
from jax import lax
from jax.experimental import pallas as pl
from jax.experimental.pallas import tpu as pltpu

D_MODEL = 1024
D_FF = 2816
N_HEADS = 8
QK_NOPE_DIM = 64
QK_ROPE_DIM = 32
QK_HEAD_DIM = QK_NOPE_DIM + QK_ROPE_DIM
V_HEAD_DIM = 64
Q_LORA_RANK = 384
KV_LORA_RANK = 256
CONV_DIM = 1024
CONV_WIDTH = 3
ROPE_THETA = 10000.0
NORM_EPS = 1e-6

LANES = 128
SUBLANES = 8
MXU_WIDTH = 256
HEAD_PAD = LANES
HALF_TILE = LANES // 2
ROPE_HALF = QK_ROPE_DIM // 2
ROPE_LO = HALF_TILE - ROPE_HALF


def _head_lane_sources():
    src = np.full((HEAD_PAD,), -1, np.int64)
    src[0:ROPE_LO] = np.arange(ROPE_LO)
    src[ROPE_LO:HALF_TILE] = QK_NOPE_DIM + np.arange(ROPE_HALF)
    rest = QK_NOPE_DIM - ROPE_LO
    src[HALF_TILE:HALF_TILE + rest] = ROPE_LO + np.arange(rest)
    src[HALF_TILE + ROPE_LO:HEAD_PAD] = QK_NOPE_DIM + ROPE_HALF + np.arange(ROPE_HALF)
    return src


HEAD_SRC = _head_lane_sources()

OFF_Q = 0
OFF_KV = OFF_Q + Q_LORA_RANK
OFF_KR = OFF_KV + KV_LORA_RANK
OFF_XC = OFF_KR + HEAD_PAD
OFF_GB = OFF_XC + CONV_DIM
OFF_GC = OFF_GB + CONV_DIM
OFF_GA_LOGIT = OFF_GC + CONV_DIM
OFF_GB_LOGIT = OFF_GA_LOGIT + D_MODEL
IN_DIM_PAD = OFF_GB_LOGIT + D_MODEL

FF_CHUNKS = (768, 768, 768, 512)
VMEM_LIMIT = 56 * 1024 * 1024

TM_FFN = 512
TM_MIX = 512
TQ = 256
MASK_VALUE = -1e30


def _bf16(x):
    return x.astype(jnp.bfloat16)


def _dot(a, b):
    return jnp.dot(a, b, preferred_element_type=jnp.float32)


def _inv_rms(x, n):
    return lax.rsqrt(jnp.sum(x * x, axis=-1, keepdims=True) * (1.0 / n) + NORM_EPS)


def _rmsnorm(x, gain, n):
    return x * _inv_rms(x, n) * gain


def _swiglu_half_step(x, gain, wg_ref, wu_ref, wd_ref, act_ref):
    h = _bf16(_rmsnorm(x, gain, D_MODEL))
    c0 = 0
    for ck in FF_CHUNKS:
        g = _dot(h, wg_ref[:, c0:c0 + ck])
        u = _dot(h, wu_ref[:, c0:c0 + ck])
        act_ref[:, c0:c0 + ck] = _bf16(g * jax.nn.sigmoid(g) * u)
        c0 += ck
    return x + 0.5 * _dot(act_ref[...], wd_ref[...])


def _ffn_kernel(x_ref, gain_ref, wg_ref, wu_ref, wd_ref, o_ref, act_ref):
    o_ref[...] = _swiglu_half_step(x_ref[...], gain_ref[...], wg_ref, wu_ref, wd_ref, act_ref)


def _swap_halves(t):
    return pltpu.roll(t, HALF_TILE, 1)


def _rope_tables(pos, inv_freq):
    half_rows = pos.shape[0] // 2
    lane = lax.broadcasted_iota(jnp.int32, (half_rows, LANES), 1)
    low = lane < HALF_TILE
    ang = jnp.where(low, pos[:half_rows], pos[half_rows:]) * inv_freq
    cos_p, sin_p = jnp.cos(ang), jnp.sin(ang)
    cos_s, sin_s = _swap_halves(cos_p), _swap_halves(sin_p)
    cos_t = jnp.concatenate([jnp.where(low, cos_p, cos_s), jnp.where(low, cos_s, cos_p)], axis=0)
    sin_t = jnp.concatenate([jnp.where(low, -sin_p, sin_s), jnp.where(low, -sin_s, sin_p)], axis=0)
    return cos_t, sin_t


def _head_norm_rope(t, cos_g, sin_g):
    t = t * _inv_rms(t, QK_HEAD_DIM)
    return t * cos_g + _swap_halves(t) * sin_g


def _mix_kernel(seq_len, x_ref, pos_ref, invf_ref, mixg_ref, win_ref, bias_ref, qag_ref, wuq_ref,
                kvag_ref, wuk_ref, wuv_ref, qhg_ref, khg_ref, convw_ref, wpc_ref,
                q_ref, k_ref, v_ref, ga_ref, gyb_ref, ubuf_ref):
    tm = x_ref.shape[0]
    h = _bf16(_rmsnorm(x_ref[...], mixg_ref[...], D_MODEL))

    def proj(off, width):
        return _dot(h, win_ref[:, off:off + width])

    @pl.when((pl.program_id(0) * tm) % seq_len == 0)
    def _():
        ubuf_ref[0:SUBLANES, :] = jnp.zeros((SUBLANES, CONV_DIM), jnp.float32)

    y_b = []
    gate_b = []

    def conv_gate_chunk(c):
        cs = slice(c * MXU_WIDTH, (c + 1) * MXU_WIDTH)
        u = proj(OFF_GC + cs.start, MXU_WIDTH) * proj(OFF_XC + cs.start, MXU_WIDTH)
        ubuf_ref[SUBLANES:SUBLANES + tm, cs] = u
        z = (convw_ref[0:1, cs] * ubuf_ref[SUBLANES - 2:SUBLANES - 2 + tm, cs]
             + convw_ref[1:2, cs] * ubuf_ref[SUBLANES - 1:SUBLANES - 1 + tm, cs]
             + convw_ref[2:3, cs] * u)
        ubuf_ref[0:SUBLANES, cs] = ubuf_ref[tm:tm + SUBLANES, cs]
        part = _dot(_bf16(proj(OFF_GB + cs.start, MXU_WIDTH) * z), wpc_ref[cs, :])
        y_b[:] = [part if not y_b else y_b[0] + part]
        ga_ref[:, cs] = _bf16(jax.nn.sigmoid(proj(OFF_GA_LOGIT + cs.start, MXU_WIDTH) + bias_ref[:, cs]))
        gate_b.append(jax.nn.sigmoid(proj(OFF_GB_LOGIT + cs.start, MXU_WIDTH)
                                     + bias_ref[:, D_MODEL + cs.start:D_MODEL + cs.stop]))

    conv_gate_chunk(0)

    cos_t, sin_t = _rope_tables(pos_ref[...], invf_ref[...])
    q_gain = qhg_ref[...] * (QK_HEAD_DIM ** -0.5)
    k_gain = khg_ref[...]
    cos_q, sin_q = cos_t * q_gain, sin_t * _swap_halves(q_gain)
    cos_k, sin_k = cos_t * k_gain, sin_t * _swap_halves(k_gain)

    qn = _bf16(_rmsnorm(proj(OFF_Q, Q_LORA_RANK), qag_ref[...], Q_LORA_RANK))
    q = _dot(qn, wuq_ref[...])
    ckv = _bf16(_rmsnorm(proj(OFF_KV, KV_LORA_RANK), kvag_ref[...], KV_LORA_RANK))
    k_nope = _dot(ckv, wuk_ref[...])
    k_rope = proj(OFF_KR, HEAD_PAD)
    v_ref[...] = _bf16(_dot(ckv, wuv_ref[...]))

    n_chunks = CONV_DIM // MXU_WIDTH
    heads_per_chunk = N_HEADS // n_chunks
    for c in range(n_chunks):
        for hd in range(c * heads_per_chunk, (c + 1) * heads_per_chunk):
            sl = slice(hd * HEAD_PAD, (hd + 1) * HEAD_PAD)
            q_ref[:, sl] = _bf16(_head_norm_rope(q[:, sl], cos_q, sin_q))
            k_ref[:, sl] = _bf16(_head_norm_rope(k_nope[:, sl] + k_rope, cos_k, sin_k))
        if c + 1 < n_chunks:
            conv_gate_chunk(c + 1)
    for c, g in enumerate(gate_b):
        cs = slice(c * MXU_WIDTH, (c + 1) * MXU_WIDTH)
        gyb_ref[:, cs] = _bf16(g * y_b[0][:, cs])


def _qk(q, k):
    return lax.dot_general(q, k, (((1,), (1,)), ((), ())), preferred_element_type=jnp.float32)


def _attn_kernel(q_ref, k_ref, v_ref, o_ref):
    s_len = q_ref.shape[0]
    lane = lax.broadcasted_iota(jnp.int32, (TQ, LANES), 1)
    row = lax.broadcasted_iota(jnp.int32, (TQ, TQ), 0)
    col = lax.broadcasted_iota(jnp.int32, (TQ, TQ), 1)
    for qi in range(s_len // TQ):
        lo, hi = qi * TQ, (qi + 1) * TQ
        outs = []
        for hd in range(2):
            hs = slice(hd * HEAD_PAD, (hd + 1) * HEAD_PAD)
            q = q_ref[lo:hi, hs]
            s_diag = jnp.where(col <= row, _qk(q, k_ref[lo:hi, hs]), MASK_VALUE)
            m = jnp.max(s_diag, axis=-1, keepdims=True)
            if qi > 0:
                s_past = _qk(q, k_ref[0:lo, hs])
                m = jnp.maximum(m, jnp.max(s_past, axis=-1, keepdims=True))
            p_diag = jnp.exp(s_diag - m)
            l = jnp.sum(p_diag, axis=-1, keepdims=True)
            acc = _dot(_bf16(p_diag), v_ref[lo:hi, :])
            if qi > 0:
                p_past = jnp.exp(s_past - m)
                l = l + jnp.sum(p_past, axis=-1, keepdims=True)
                acc = acc + _dot(_bf16(p_past), v_ref[0:lo, :])
            outs.append(acc / l)
        o_ref[lo:hi, :] = _bf16(jnp.where(lane < V_HEAD_DIM, outs[0], outs[1]))


def _post_kernel(x_ref, attn_ref, ga_ref, gyb_ref, wpa_ref, wout_ref, gain_ref, wg_ref, wu_ref, wd_ref,
                 o_ref, act_ref):
    y_a = _dot(attn_ref[...], wpa_ref[...])
    merged = ga_ref[...].astype(jnp.float32) * y_a + gyb_ref[...].astype(jnp.float32)
    x2 = x_ref[...] + _dot(_bf16(merged), wout_ref[...])
    o_ref[...] = _swiglu_half_step(x2, gain_ref[...], wg_ref, wu_ref, wd_ref, act_ref)


def _rows(tm, width):
    return pl.BlockSpec((tm, width), lambda i: (i, 0))


def _resident():
    return pl.BlockSpec(memory_space=pltpu.VMEM)


def _params(n_axes):
    return pltpu.CompilerParams(dimension_semantics=("arbitrary",) * n_axes, vmem_limit_bytes=VMEM_LIMIT)


def _to_head_tiles(w, real):
    kdim = w.shape[0]
    w = w.reshape(kdim, -1, real)
    w = jnp.pad(w, ((0, 0), (0, 0), (0, 1)))
    src = np.where((HEAD_SRC >= 0) & (HEAD_SRC < real), HEAD_SRC, real)
    return w[:, :, src].reshape(kdim, -1)


def kernel(x, positions, ffn1_norm, ffn1_w_gate, ffn1_w_up, ffn1_w_down, mix_norm, w_in, gate_bias, q_a_norm, w_uq, kv_a_norm, w_uk, w_uv, q_head_norm, k_head_norm, w_proj_attn, conv_w, w_proj_conv, w_out, ffn2_norm, ffn2_w_gate, ffn2_w_up, ffn2_w_down):
    b, s, d = x.shape
    t = b * s
    assert d == D_MODEL and t % TM_FFN == 0 and s % TM_MIX == 0 and s % TQ == 0
    f32 = jnp.float32
    bf = jnp.bfloat16
    row = lambda g: g.reshape(1, -1).astype(f32)

    x_flat = x.reshape(t, d)
    pos = positions.reshape(t, 1).astype(f32)

    sp = (Q_LORA_RANK, Q_LORA_RANK + KV_LORA_RANK, Q_LORA_RANK + KV_LORA_RANK + QK_ROPE_DIM)
    w_kr = jnp.pad(w_in[:, sp[1]:sp[2]], ((0, 0), (QK_NOPE_DIM, 0)))
    w_in_pad = _bf16(jnp.concatenate([w_in[:, :sp[1]], _to_head_tiles(w_kr, QK_HEAD_DIM), w_in[:, sp[2]:]], axis=1))
    assert w_in_pad.shape[1] == IN_DIM_PAD
    w_uq_pad = _bf16(_to_head_tiles(w_uq, QK_HEAD_DIM))
    w_uk_pad = _bf16(_to_head_tiles(w_uk, QK_NOPE_DIM))
    head_gain = lambda g: _to_head_tiles(g.reshape(1, QK_HEAD_DIM).astype(f32), QK_HEAD_DIM)
    inv_freq = 1.0 / (ROPE_THETA ** (jnp.arange(ROPE_HALF, dtype=f32) / ROPE_HALF))
    inv_freq_pad = _to_head_tiles(
        jnp.concatenate([jnp.zeros((QK_NOPE_DIM,), f32), inv_freq, inv_freq]).reshape(1, QK_HEAD_DIM), QK_HEAD_DIM)

    def ffn_weights(wg, wu, wd):
        return _bf16(wg), _bf16(wu), _bf16(wd)

    x1 = pl.pallas_call(
        _ffn_kernel,
        grid=(t // TM_FFN,),
        in_specs=[_rows(TM_FFN, d)] + [_resident()] * 4,
        out_specs=_rows(TM_FFN, d),
        out_shape=jax.ShapeDtypeStruct((t, d), f32),
        scratch_shapes=[pltpu.VMEM((TM_FFN, D_FF), bf)],
        compiler_params=_params(1),
        name="ffn1",
    )(x_flat, row(ffn1_norm), *ffn_weights(ffn1_w_gate, ffn1_w_up, ffn1_w_down))

    q, k, v, gate_a, gated_yb = pl.pallas_call(
        functools.partial(_mix_kernel, s),
        grid=(t // TM_MIX,),
        in_specs=[_rows(TM_MIX, d), _rows(TM_MIX, 1)] + [_resident()] * 13,
        out_specs=[_rows(TM_MIX, N_HEADS * HEAD_PAD), _rows(TM_MIX, N_HEADS * HEAD_PAD),
                   _rows(TM_MIX, N_HEADS * V_HEAD_DIM), _rows(TM_MIX, d), _rows(TM_MIX, d)],
        out_shape=[jax.ShapeDtypeStruct((t, N_HEADS * HEAD_PAD), bf), jax.ShapeDtypeStruct((t, N_HEADS * HEAD_PAD), bf),
                   jax.ShapeDtypeStruct((t, N_HEADS * V_HEAD_DIM), bf), jax.ShapeDtypeStruct((t, d), bf),
                   jax.ShapeDtypeStruct((t, d), bf)],
        scratch_shapes=[pltpu.VMEM((TM_MIX + 2 * SUBLANES, CONV_DIM), f32)],
        compiler_params=_params(1),
        name="mix",
    )(x1, pos, inv_freq_pad, row(mix_norm), w_in_pad, row(gate_bias), row(q_a_norm), w_uq_pad,
      row(kv_a_norm), w_uk_pad, _bf16(w_uv), head_gain(q_head_norm), head_gain(k_head_norm),
      conv_w.astype(f32), _bf16(w_proj_conv))

    pair = 2 * HEAD_PAD
    attn = pl.pallas_call(
        _attn_kernel,
        grid=(b, N_HEADS // 2),
        in_specs=[pl.BlockSpec((None, s, pair), lambda bi, hp: (bi, 0, hp)),
                  pl.BlockSpec((None, s, pair), lambda bi, hp: (bi, 0, hp)),
                  pl.BlockSpec((None, s, LANES), lambda bi, hp: (bi, 0, hp))],
        out_specs=pl.BlockSpec((None, s, LANES), lambda bi, hp: (bi, 0, hp)),
        out_shape=jax.ShapeDtypeStruct((b, s, N_HEADS * V_HEAD_DIM), bf),
        compiler_params=_params(2),
        name="attn",
    )(q.reshape(b, s, -1), k.reshape(b, s, -1), v.reshape(b, s, -1))

    out = pl.pallas_call(
        _post_kernel,
        grid=(t // TM_FFN,),
        in_specs=[_rows(TM_FFN, d), _rows(TM_FFN, N_HEADS * V_HEAD_DIM), _rows(TM_FFN, d), _rows(TM_FFN, d)]
        + [_resident()] * 6,
        out_specs=_rows(TM_FFN, d),
        out_shape=jax.ShapeDtypeStruct((t, d), f32),
        scratch_shapes=[pltpu.VMEM((TM_FFN, D_FF), bf)],
        compiler_params=_params(1),
        name="post",
    )(x1, attn.reshape(t, -1), gate_a, gated_yb, _bf16(w_proj_attn), _bf16(w_out), row(ffn2_norm),
      *ffn_weights(ffn2_w_gate, ffn2_w_up, ffn2_w_down))
    return out.reshape(b, s, d)
```

```python
import functools

import jax
import jax.numpy as jnp
import numpy as np
from jax import lax
from jax.experimental import pallas as pl
from jax.experimental.pallas import tpu as pltpu

D_MODEL = 1024
D_FF = 2816
N_HEADS = 8
QK_NOPE_DIM = 64
QK_ROPE_DIM = 32
QK_HEAD_DIM = QK_NOPE_DIM + QK_ROPE_DIM
V_HEAD_DIM = 64
Q_LORA_RANK = 384
KV_LORA_RANK = 256
CONV_DIM = 1024
CONV_WIDTH = 3
ROPE_THETA = 10000.0
NORM_EPS = 1e-6

LANES = 128
SUBLANES = 8
MXU_WIDTH = 256
HEAD_PAD = LANES
HALF_TILE = LANES // 2
ROPE_HALF = QK_ROPE_DIM // 2
ROPE_LO = HALF_TILE - ROPE_HALF


def _head_lane_sources():
    src = np.full((HEAD_PAD,), -1, np.int64)
    src[0:ROPE_LO] = np.arange(ROPE_LO)
    src[ROPE_LO:HALF_TILE] = QK_NOPE_DIM + np.arange(ROPE_HALF)
    rest = QK_NOPE_DIM - ROPE_LO
    src[HALF_TILE:HALF_TILE + rest] = ROPE_LO + np.arange(rest)
    src[HALF_TILE + ROPE_LO:HEAD_PAD] = QK_NOPE_DIM + ROPE_HALF + np.arange(ROPE_HALF)
    return src


HEAD_SRC = _head_lane_sources()

OFF_Q = 0
OFF_KV = OFF_Q + Q_LORA_RANK
OFF_KR = OFF_KV + KV_LORA_RANK
OFF_XC = OFF_KR + HEAD_PAD
OFF_GB = OFF_XC + CONV_DIM
OFF_GC = OFF_GB + CONV_DIM
OFF_GA_LOGIT = OFF_GC + CONV_DIM
OFF_GB_LOGIT = OFF_GA_LOGIT + D_MODEL
IN_DIM_PAD = OFF_GB_LOGIT + D_MODEL

FF_CHUNKS = (768, 768, 768, 512)
VMEM_LIMIT = 56 * 1024 * 1024

TM_FFN = 512
TM_MIX = 512
TQ = 256
SCORE_LOOKAHEAD = 3
MASK_VALUE = -1e30
LOG2_E = 1.4426950408889634


def _bf16(x):
    return x.astype(jnp.bfloat16)


def _dot(a, b):
    return jnp.dot(a, b, preferred_element_type=jnp.float32)


def _dot_nt(a, b):
    return lax.dot_general(a, b, (((1,), (1,)), ((), ())), preferred_element_type=jnp.float32)


def _inv_rms(x, n):
    return lax.rsqrt(jnp.sum(x * x, axis=-1, keepdims=True) * (1.0 / n) + NORM_EPS)


def _rmsnorm(x, gain, n):
    return x * _inv_rms(x, n) * gain


def _swiglu_half_step(x, gain, wg_ref, wu_ref, wd_ref, act_ref):
    h = _bf16(_rmsnorm(x, gain, D_MODEL))
    c0 = 0
    for ck in FF_CHUNKS:
        g = _dot(h, wg_ref[:, c0:c0 + ck])
        u = _dot(h, wu_ref[:, c0:c0 + ck])
        act_ref[:, c0:c0 + ck] = _bf16(g * jax.nn.sigmoid(g) * u)
        c0 += ck
    return x + 0.5 * _dot(act_ref[...], wd_ref[...])


def _ffn_kernel(x_ref, gain_ref, wg_ref, wu_ref, wd_ref, o_ref, act_ref):
    o_ref[...] = _swiglu_half_step(x_ref[...], gain_ref[...], wg_ref, wu_ref, wd_ref, act_ref)


def _swap_halves(t):
    return pltpu.roll(t, HALF_TILE, 1)


def _rope_tables(pos, inv_freq):
    half_rows = pos.shape[0] // 2
    lane = lax.broadcasted_iota(jnp.int32, (half_rows, LANES), 1)
    low = lane < HALF_TILE
    ang = jnp.where(low, pos[:half_rows], pos[half_rows:]) * inv_freq
    cos_p, sin_p = jnp.cos(ang), jnp.sin(ang)
    cos_s, sin_s = _swap_halves(cos_p), _swap_halves(sin_p)
    cos_t = jnp.concatenate([jnp.where(low, cos_p, cos_s), jnp.where(low, cos_s, cos_p)], axis=0)
    sin_t = jnp.concatenate([jnp.where(low, -sin_p, sin_s), jnp.where(low, -sin_s, sin_p)], axis=0)
    return cos_t, sin_t


def _head_norm_rope(t, cos_g, sin_g):
    t = t * _inv_rms(t, QK_HEAD_DIM)
    return t * cos_g + _swap_halves(t) * sin_g


def _mix_kernel(seq_len, x_ref, pos_ref, invf_ref, mixg_ref, win_ref, bias_ref, qag_ref, wuq_ref,
                kvag_ref, wuk_ref, wuvt_ref, vone_ref, qhg_ref, khg_ref, convw_ref, wpc_ref,
                q_ref, k_ref, vt_ref, ga_ref, gyb_ref, ubuf_ref):
    tm = x_ref.shape[0]
    h = _bf16(_rmsnorm(x_ref[...], mixg_ref[...], D_MODEL))

    def proj(off, width):
        return _dot(h, win_ref[:, off:off + width])

    @pl.when((pl.program_id(0) * tm) % seq_len == 0)
    def _():
        ubuf_ref[0:SUBLANES, :] = jnp.zeros((SUBLANES, CONV_DIM), jnp.float32)

    y_b = []
    gate_b = []

    def conv_gate_chunk(c):
        cs = slice(c * MXU_WIDTH, (c + 1) * MXU_WIDTH)
        u = proj(OFF_GC + cs.start, MXU_WIDTH) * proj(OFF_XC + cs.start, MXU_WIDTH)
        ubuf_ref[SUBLANES:SUBLANES + tm, cs] = u
        z = (convw_ref[0:1, cs] * ubuf_ref[SUBLANES - 2:SUBLANES - 2 + tm, cs]
             + convw_ref[1:2, cs] * ubuf_ref[SUBLANES - 1:SUBLANES - 1 + tm, cs]
             + convw_ref[2:3, cs] * u)
        ubuf_ref[0:SUBLANES, cs] = ubuf_ref[tm:tm + SUBLANES, cs]
        part = _dot(_bf16(proj(OFF_GB + cs.start, MXU_WIDTH) * z), wpc_ref[cs, :])
        y_b[:] = [part if not y_b else y_b[0] + part]
        ga_ref[:, cs] = _bf16(jax.nn.sigmoid(proj(OFF_GA_LOGIT + cs.start, MXU_WIDTH) + bias_ref[:, cs]))
        gate_b.append(jax.nn.sigmoid(proj(OFF_GB_LOGIT + cs.start, MXU_WIDTH)
                                     + bias_ref[:, D_MODEL + cs.start:D_MODEL + cs.stop]))

    conv_gate_chunk(0)

    cos_t, sin_t = _rope_tables(pos_ref[...], invf_ref[...])
    q_gain = qhg_ref[...] * (QK_HEAD_DIM ** -0.5 * LOG2_E)
    k_gain = khg_ref[...]
    cos_q, sin_q = cos_t * q_gain, sin_t * _swap_halves(q_gain)
    cos_k, sin_k = cos_t * k_gain, sin_t * _swap_halves(k_gain)

    qn = _bf16(_rmsnorm(proj(OFF_Q, Q_LORA_RANK), qag_ref[...], Q_LORA_RANK))
    q = _dot(qn, wuq_ref[...])
    ckv = _bf16(_rmsnorm(proj(OFF_KV, KV_LORA_RANK), kvag_ref[...], KV_LORA_RANK))
    k_nope = _dot(ckv, wuk_ref[...])
    k_rope = proj(OFF_KR, HEAD_PAD)
    vt_ref[...] = _bf16(_dot_nt(wuvt_ref[...], ckv) + vone_ref[...])

    n_chunks = CONV_DIM // MXU_WIDTH
    heads_per_chunk = N_HEADS // n_chunks
    for c in range(n_chunks):
        for hd in range(c * heads_per_chunk, (c + 1) * heads_per_chunk):
            sl = slice(hd * HEAD_PAD, (hd + 1) * HEAD_PAD)
            q_ref[:, sl] = _bf16(_head_norm_rope(q[:, sl], cos_q, sin_q))
            k_ref[:, sl] = _bf16(_head_norm_rope(k_nope[:, sl] + k_rope, cos_k, sin_k))
        if c + 1 < n_chunks:
            conv_gate_chunk(c + 1)
    for c, g in enumerate(gate_b):
        cs = slice(c * MXU_WIDTH, (c + 1) * MXU_WIDTH)
        gyb_ref[:, cs] = _bf16(g * y_b[0][:, cs])


def _attn_kernel(q_ref, k_ref, vt_ref, o_ref):
    s_len = q_ref.shape[0]
    key = lax.broadcasted_iota(jnp.int32, (TQ, TQ), 0)
    qry = lax.broadcasted_iota(jnp.int32, (TQ, TQ), 1)
    n_q = s_len // TQ

    def scores(qi, hd):
        lo, hi = qi * TQ, (qi + 1) * TQ
        hs = slice(hd * HEAD_PAD, (hd + 1) * HEAD_PAD)
        return _dot_nt(k_ref[0:hi, hs], q_ref[lo:hi, hs])

    def finish(qi, hd, s):
        lo, hi = qi * TQ, (qi + 1) * TQ
        hs = slice(hd * HEAD_PAD, (hd + 1) * HEAD_PAD)
        s_diag = jnp.where(key <= qry, s[lo:hi, :], MASK_VALUE)
        s = s_diag if qi == 0 else jnp.concatenate([s[0:lo, :], s_diag], axis=0)
        p = jnp.exp2(s - jnp.max(s, axis=0, keepdims=True))
        acc = _dot(vt_ref[hs, 0:hi], _bf16(p))
        return acc[0:V_HEAD_DIM, :] / acc[V_HEAD_DIM:V_HEAD_DIM + 1, :]

    chains = [(qi, hd) for qi in range(n_q) for hd in range(2)]
    pending = [scores(*c) for c in chains[:SCORE_LOOKAHEAD]]
    outs = []
    for i, (qi, hd) in enumerate(chains):
        s = pending.pop(0)
        if i + SCORE_LOOKAHEAD < len(chains):
            pending.append(scores(*chains[i + SCORE_LOOKAHEAD]))
        outs.append(finish(qi, hd, s))
        if hd == 1:
            o_ref[qi * TQ:(qi + 1) * TQ, :] = _bf16(jnp.concatenate(outs, axis=0).T)
            outs = []


def _post_kernel(x_ref, attn_ref, ga_ref, gyb_ref, wpa_ref, wout_ref, gain_ref, wg_ref, wu_ref, wd_ref,
                 o_ref, act_ref):
    y_a = _dot(attn_ref[...], wpa_ref[...])
    merged = ga_ref[...].astype(jnp.float32) * y_a + gyb_ref[...].astype(jnp.float32)
    x2 = x_ref[...] + _dot(_bf16(merged), wout_ref[...])
    o_ref[...] = _swiglu_half_step(x2, gain_ref[...], wg_ref, wu_ref, wd_ref, act_ref)


def _rows(tm, width):
    return pl.BlockSpec((tm, width), lambda i: (i, 0))


def _resident():
    return pl.BlockSpec(memory_space=pltpu.VMEM)


def _params(n_axes):
    return pltpu.CompilerParams(dimension_semantics=("arbitrary",) * n_axes, vmem_limit_bytes=VMEM_LIMIT)


def _to_head_tiles(w, real):
    kdim = w.shape[0]
    w = w.reshape(kdim, -1, real)
    w = jnp.pad(w, ((0, 0), (0, 0), (0, 1)))
    src = np.where((HEAD_SRC >= 0) & (HEAD_SRC < real), HEAD_SRC, real)
    return w[:, :, src].reshape(kdim, -1)


def kernel(x, positions, ffn1_norm, ffn1_w_gate, ffn1_w_up, ffn1_w_down, mix_norm, w_in, gate_bias, q_a_norm, w_uq, kv_a_norm, w_uk, w_uv, q_head_norm, k_head_norm, w_proj_attn, conv_w, w_proj_conv, w_out, ffn2_norm, ffn2_w_gate, ffn2_w_up, ffn2_w_down):
    b, s, d = x.shape
    t = b * s
    assert d == D_MODEL and t % TM_FFN == 0 and s % TM_MIX == 0 and s % TQ == 0
    f32 = jnp.float32
    bf = jnp.bfloat16
    row = lambda g: g.reshape(1, -1).astype(f32)

    x_flat = x.reshape(t, d)
    pos = positions.reshape(t, 1).astype(f32)

    sp = (Q_LORA_RANK, Q_LORA_RANK + KV_LORA_RANK, Q_LORA_RANK + KV_LORA_RANK + QK_ROPE_DIM)
    w_kr = jnp.pad(w_in[:, sp[1]:sp[2]], ((0, 0), (QK_NOPE_DIM, 0)))
    w_in_pad = _bf16(jnp.concatenate([w_in[:, :sp[1]], _to_head_tiles(w_kr, QK_HEAD_DIM), w_in[:, sp[2]:]], axis=1))
    assert w_in_pad.shape[1] == IN_DIM_PAD
    w_uq_pad = _bf16(_to_head_tiles(w_uq, QK_HEAD_DIM))
    w_uk_pad = _bf16(_to_head_tiles(w_uk, QK_NOPE_DIM))
    v_pad = ((0, 0), (0, 0), (0, HEAD_PAD - V_HEAD_DIM))
    w_uv_t = _bf16(jnp.pad(w_uv.reshape(KV_LORA_RANK, N_HEADS, V_HEAD_DIM), v_pad).reshape(KV_LORA_RANK, -1).T)
    v_ones = jnp.zeros((N_HEADS, HEAD_PAD, 1), f32).at[:, V_HEAD_DIM].set(1.0).reshape(-1, 1)
    head_gain = lambda g: _to_head_tiles(g.reshape(1, QK_HEAD_DIM).astype(f32), QK_HEAD_DIM)
    inv_freq = 1.0 / (ROPE_THETA ** (jnp.arange(ROPE_HALF, dtype=f32) / ROPE_HALF))
    inv_freq_pad = _to_head_tiles(
        jnp.concatenate([jnp.zeros((QK_NOPE_DIM,), f32), inv_freq, inv_freq]).reshape(1, QK_HEAD_DIM), QK_HEAD_DIM)

    def ffn_weights(wg, wu, wd):
        return _bf16(wg), _bf16(wu), _bf16(wd)

    x1 = pl.pallas_call(
        _ffn_kernel,
        grid=(t // TM_FFN,),
        in_specs=[_rows(TM_FFN, d)] + [_resident()] * 4,
        out_specs=_rows(TM_FFN, d),
        out_shape=jax.ShapeDtypeStruct((t, d), f32),
        scratch_shapes=[pltpu.VMEM((TM_FFN, D_FF), bf)],
        compiler_params=_params(1),
        name="ffn1",
    )(x_flat, row(ffn1_norm), *ffn_weights(ffn1_w_gate, ffn1_w_up, ffn1_w_down))

    heads_w = N_HEADS * HEAD_PAD
    q, k, v_t, gate_a, gated_yb = pl.pallas_call(
        functools.partial(_mix_kernel, s),
        grid=(t // TM_MIX,),
        in_specs=[_rows(TM_MIX, d), _rows(TM_MIX, 1)] + [_resident()] * 14,
        out_specs=[_rows(TM_MIX, heads_w), _rows(TM_MIX, heads_w), pl.BlockSpec((heads_w, TM_MIX), lambda i: (0, i)),
                   _rows(TM_MIX, d), _rows(TM_MIX, d)],
        out_shape=[jax.ShapeDtypeStruct((t, heads_w), bf), jax.ShapeDtypeStruct((t, heads_w), bf),
                   jax.ShapeDtypeStruct((heads_w, t), bf), jax.ShapeDtypeStruct((t, d), bf),
                   jax.ShapeDtypeStruct((t, d), bf)],
        scratch_shapes=[pltpu.VMEM((TM_MIX + 2 * SUBLANES, CONV_DIM), f32)],
        compiler_params=_params(1),
        name="mix",
    )(x1, pos, inv_freq_pad, row(mix_norm), w_in_pad, row(gate_bias), row(q_a_norm), w_uq_pad,
      row(kv_a_norm), w_uk_pad, w_uv_t, v_ones, head_gain(q_head_norm), head_gain(k_head_norm),
      conv_w.astype(f32), _bf16(w_proj_conv))

    pair = 2 * HEAD_PAD
    attn = pl.pallas_call(
        _attn_kernel,
        grid=(b, N_HEADS // 2),
        in_specs=[pl.BlockSpec((None, s, pair), lambda bi, hp: (bi, 0, hp)),
                  pl.BlockSpec((None, s, pair), lambda bi, hp: (bi, 0, hp)),
                  pl.BlockSpec((pair, s), lambda bi, hp: (hp, bi))],
        out_specs=pl.BlockSpec((None, s, LANES), lambda bi, hp: (bi, 0, hp)),
        out_shape=jax.ShapeDtypeStruct((b, s, N_HEADS * V_HEAD_DIM), bf),
        compiler_params=_params(2),
        name="attn",
    )(q.reshape(b, s, -1), k.reshape(b, s, -1), v_t)

    out = pl.pallas_call(
        _post_kernel,
        grid=(t // TM_FFN,),
        in_specs=[_rows(TM_FFN, d), _rows(TM_FFN, N_HEADS * V_HEAD_DIM), _rows(TM_FFN, d), _rows(TM_FFN, d)]
        + [_resident()] * 6,
        out_specs=_rows(TM_FFN, d),
        out_shape=jax.ShapeDtypeStruct((t, d), f32),
        scratch_shapes=[pltpu.VMEM((TM_FFN, D_FF), bf)],
        compiler_params=_params(1),
        name="post",
    )(x1, attn.reshape(t, -1), gate_a, gated_yb, _bf16(w_proj_attn), _bf16(w_out), row(ffn2_norm),
      *ffn_weights(ffn2_w_gate, ffn2_w_up, ffn2_w_down))
    return out.reshape(b, s, d)
```

```python
import functools

import jax
import jax.numpy as jnp
import numpy as np
from jax import lax
from jax.experimental import pallas as pl
from jax.experimental.pallas import tpu as pltpu

D_MODEL = 1024
D_FF = 2816
N_HEADS = 8
QK_NOPE_DIM = 64
QK_ROPE_DIM = 32
QK_HEAD_DIM = QK_NOPE_DIM + QK_ROPE_DIM
V_HEAD_DIM = 64
Q_LORA_RANK = 384
KV_LORA_RANK = 256
CONV_DIM = 1024
CONV_WIDTH = 3
ROPE_THETA = 10000.0
NORM_EPS = 1e-6

LANES = 128
SUBLANES = 8
MXU_WIDTH = 256
HEAD_PAD = LANES
HALF_TILE = LANES // 2
ROPE_HALF = QK_ROPE_DIM // 2
ROPE_LO = HALF_TILE - ROPE_HALF


def _head_feature_sources():
    src = np.full((HEAD_PAD,), -1, np.int64)
    src[0:ROPE_LO] = np.arange(ROPE_LO)
    src[ROPE_LO:HALF_TILE] = QK_NOPE_DIM + np.arange(ROPE_HALF)
    rest = QK_NOPE_DIM - ROPE_LO
    src[HALF_TILE:HALF_TILE + rest] = ROPE_LO + np.arange(rest)
    src[HALF_TILE + ROPE_LO:HEAD_PAD] = QK_NOPE_DIM + ROPE_HALF + np.arange(ROPE_HALF)
    return src


HEAD_SRC = _head_feature_sources()

LATENT_DIM = Q_LORA_RANK + KV_LORA_RANK
REST_START = LATENT_DIM + QK_ROPE_DIM
OFF_XC = 0
OFF_GB = OFF_XC + CONV_DIM
OFF_GC = OFF_GB + CONV_DIM
OFF_GA_LOGIT = OFF_GC + CONV_DIM
OFF_GB_LOGIT = OFF_GA_LOGIT + D_MODEL

FF_CHUNKS = (768, 768, 768, 512)
VMEM_LIMIT = 56 * 1024 * 1024

TM_FFN = 512
TM_FFN1 = 1024
TM_MIX = 512
TQ = 256
PROJ_AHEAD = 1
SCORE_LOOKAHEAD = 3
MASK_VALUE = -1e30
LOG2_E = 1.4426950408889634


def _bf16(x):
    return x.astype(jnp.bfloat16)


def _dot(a, b):
    return jnp.dot(a, b, preferred_element_type=jnp.float32)


def _dot_nt(a, b):
    return lax.dot_general(a, b, (((1,), (1,)), ((), ())), preferred_element_type=jnp.float32)


def _sum_sq(x):
    return jnp.sum(x * x, axis=-1, keepdims=True)


def _rmsnorm(x, gain, n):
    return x * lax.rsqrt(_sum_sq(x) * (1.0 / n) + NORM_EPS) * gain


def _swiglu_half_step(x, gain, wg_ref, wu_ref, wd_ref, act_ref):
    h = _bf16(_rmsnorm(x, gain, D_MODEL))
    c0 = 0
    for ck in FF_CHUNKS:
        g = _dot(h, wg_ref[:, c0:c0 + ck])
        u = _dot(h, wu_ref[:, c0:c0 + ck])
        act_ref[:, c0:c0 + ck] = _bf16(g * jax.nn.sigmoid(g) * u)
        c0 += ck
    return x + 0.5 * _dot(act_ref[...], wd_ref[...])


def _ffn_kernel(x_ref, gain_ref, wg_ref, wu_ref, wd_ref, o_ref, act_ref):
    o_ref[...] = _swiglu_half_step(x_ref[...], gain_ref[...], wg_ref, wu_ref, wd_ref, act_ref)


def _swap_halves(t):
    return pltpu.roll(t, HALF_TILE, 1)


def _rope_tables_token_major(pos, inv_freq):
    half_rows = pos.shape[0] // 2
    lane = lax.broadcasted_iota(jnp.int32, (half_rows, LANES), 1)
    low = lane < HALF_TILE
    ang = jnp.where(low, pos[:half_rows], pos[half_rows:]) * inv_freq
    cos_p, sin_p = jnp.cos(ang), jnp.sin(ang)
    cos_s, sin_s = _swap_halves(cos_p), _swap_halves(sin_p)
    cos_t = jnp.concatenate([jnp.where(low, cos_p, cos_s), jnp.where(low, cos_s, cos_p)], axis=0)
    sin_t = jnp.concatenate([jnp.where(low, -sin_p, sin_s), jnp.where(low, -sin_s, sin_p)], axis=0)
    return cos_t, sin_t


def _q_head_feature_major(x, gain, cos, sin):
    inv = lax.rsqrt(jnp.sum(x * x, axis=0, keepdims=True) * (1.0 / QK_HEAD_DIM) + NORM_EPS)
    y = x * gain
    lo2 = HALF_TILE + ROPE_LO
    y1, y2 = y[ROPE_LO:HALF_TILE], y[lo2:HEAD_PAD]
    rotated = jnp.concatenate([y[0:ROPE_LO], y1 * cos - y2 * sin, y[HALF_TILE:lo2], y2 * cos + y1 * sin], axis=0)
    return rotated * inv


def _mix_kernel(seq_len, x_ref, pos_col_ref, pos_row_ref, invf_row_ref, invf_col_ref, mixg_ref, wlat_ref, wkr_ref,
                wrest_ref, bias_ref,
                qag_ref, wuqt_ref, kvag_ref, wuk_ref, wuvt_ref, vone_ref, qhg_ref, khg_ref, convw_ref, wpc_ref,
                qt_ref, k_ref, vt_ref, ga_ref, gyb_ref, ubuf_ref):
    tm = x_ref.shape[0]
    h = _bf16(_rmsnorm(x_ref[...], mixg_ref[...], D_MODEL))

    def proj(off, width):
        return _dot(h, wrest_ref[:, off:off + width])

    @pl.when((pl.program_id(0) * tm) % seq_len == 0)
    def _():
        ubuf_ref[0:SUBLANES, :] = jnp.zeros((SUBLANES, CONV_DIM), jnp.float32)

    n_chunks = CONV_DIM // MXU_WIDTH
    heads_per_chunk = N_HEADS // n_chunks
    y_b = []
    gate_b = []

    def conv_projections(c):
        return tuple(proj(off + c * MXU_WIDTH, MXU_WIDTH) for off in (OFF_GC, OFF_XC, OFF_GB))

    def conv_chunk(c, projections):
        p_gc, p_xc, p_gb = projections
        cs = slice(c * MXU_WIDTH, (c + 1) * MXU_WIDTH)
        u = p_gc * p_xc
        ubuf_ref[SUBLANES:SUBLANES + tm, cs] = u
        z = (convw_ref[0:1, cs] * ubuf_ref[SUBLANES - 2:SUBLANES - 2 + tm, cs]
             + convw_ref[1:2, cs] * ubuf_ref[SUBLANES - 1:SUBLANES - 1 + tm, cs]
             + convw_ref[2:3, cs] * u)
        ubuf_ref[0:SUBLANES, cs] = ubuf_ref[tm:tm + SUBLANES, cs]
        part = _dot(_bf16(p_gb * z), wpc_ref[cs, :])
        y_b[:] = [part if not y_b else y_b[0] + part]

    def gate_projections(c):
        return tuple(proj(off + c * MXU_WIDTH, MXU_WIDTH) for off in (OFF_GA_LOGIT, OFF_GB_LOGIT))

    def gate_chunk(c, projections):
        p_ga, p_gbl = projections
        cs = slice(c * MXU_WIDTH, (c + 1) * MXU_WIDTH)
        ga_ref[:, cs] = _bf16(jax.nn.sigmoid(p_ga + bias_ref[:, cs]))
        gate_b.append(jax.nn.sigmoid(p_gbl + bias_ref[:, D_MODEL + cs.start:D_MODEL + cs.stop]))

    def q_heads(tokens, q_t):
        for hd in range(N_HEADS):
            sl = slice(hd * HEAD_PAD, (hd + 1) * HEAD_PAD)
            qt_ref[sl, tokens] = _bf16(_q_head_feature_major(
                q_t[sl, :], q_gain[:, tokens], cos_q[:, tokens], sin_q[:, tokens]))

    def k_heads(c, k_nope):
        for i in range(heads_per_chunk):
            kn = k_nope[:, i * HEAD_PAD:(i + 1) * HEAD_PAD]
            inv = lax.rsqrt((_sum_sq(kn) + k_rope_sq) * (1.0 / QK_HEAD_DIM) + NORM_EPS)
            hd = c * heads_per_chunk + i
            k_ref[:, hd * HEAD_PAD:(hd + 1) * HEAD_PAD] = _bf16((kn * k_gain + k_rope_rot) * inv)

    q_lat = _dot(h, wlat_ref[:, 0:Q_LORA_RANK])
    kv_lat = _dot(h, wlat_ref[:, Q_LORA_RANK:LATENT_DIM])
    k_rope = _dot(h, wkr_ref[...])
    conv_p = [conv_projections(c) for c in range(PROJ_AHEAD)]

    ang_q = invf_col_ref[...] * pos_row_ref[...]
    cos_q, sin_q = jnp.cos(ang_q), jnp.sin(ang_q)
    q_gain = qhg_ref[...] * (QK_HEAD_DIM ** -0.5 * LOG2_E)
    cos_k, sin_k = _rope_tables_token_major(pos_col_ref[...], invf_row_ref[...])
    k_gain = khg_ref[...]
    k_rope_g = k_rope * k_gain
    k_rope_rot = k_rope_g * cos_k + _swap_halves(k_rope_g) * sin_k
    k_rope_sq = _sum_sq(k_rope)

    qn = _bf16(_rmsnorm(q_lat, qag_ref[...], Q_LORA_RANK))
    ckv = _bf16(_rmsnorm(kv_lat, kvag_ref[...], KV_LORA_RANK))
    gate_p = []
    half = tm // 2
    for c in range(n_chunks):
        if c % 2 == 0:
            tokens = slice((c // 2) * half, (c // 2 + 1) * half)
            q_heads(tokens, _dot_nt(wuqt_ref[...], qn[tokens]))
        k_heads(c, _dot(ckv, wuk_ref[:, c * MXU_WIDTH:(c + 1) * MXU_WIDTH]))
        if c + PROJ_AHEAD < n_chunks:
            conv_p.append(conv_projections(c + PROJ_AHEAD))
        gate_p.append(gate_projections(c))
    vt_ref[...] = _bf16(_dot_nt(wuvt_ref[...], ckv) + vone_ref[...])

    for c in range(n_chunks):
        conv_chunk(c, conv_p[c])
        gate_chunk(c, gate_p[c])
    for c, g in enumerate(gate_b):
        cs = slice(c * MXU_WIDTH, (c + 1) * MXU_WIDTH)
        gyb_ref[:, cs] = _bf16(g * y_b[0][:, cs])


def _attn_kernel(qt_ref, k_ref, vt_ref, o_ref):
    s_len = k_ref.shape[0]
    key = lax.broadcasted_iota(jnp.int32, (TQ, TQ), 0)
    qry = lax.broadcasted_iota(jnp.int32, (TQ, TQ), 1)
    n_q = s_len // TQ

    def scores(qi, hd):
        lo, hi = qi * TQ, (qi + 1) * TQ
        hs = slice(hd * HEAD_PAD, (hd + 1) * HEAD_PAD)
        return _dot(k_ref[0:hi, hs], qt_ref[hs, lo:hi])

    def finish(qi, hd, s):
        lo, hi = qi * TQ, (qi + 1) * TQ
        hs = slice(hd * HEAD_PAD, (hd + 1) * HEAD_PAD)
        s_diag = jnp.where(key <= qry, s[lo:hi, :], MASK_VALUE)
        s = s_diag if qi == 0 else jnp.concatenate([s[0:lo, :], s_diag], axis=0)
        p = jnp.exp2(s - jnp.max(s, axis=0, keepdims=True))
        acc = _dot(vt_ref[hs, 0:hi], _bf16(p))
        return acc[0:V_HEAD_DIM, :] / acc[V_HEAD_DIM:V_HEAD_DIM + 1, :]

    chains = [(qi, hd) for qi in range(n_q) for hd in range(2)]
    pending = [scores(*c) for c in chains[:SCORE_LOOKAHEAD]]
    outs = []
    for i, (qi, hd) in enumerate(chains):
        s = pending.pop(0)
        if i + SCORE_LOOKAHEAD < len(chains):
            pending.append(scores(*chains[i + SCORE_LOOKAHEAD]))
        outs.append(finish(qi, hd, s))
        if hd == 1:
            o_ref[qi * TQ:(qi + 1) * TQ, :] = _bf16(jnp.concatenate(outs, axis=0).T)
            outs = []


def _post_kernel(x_ref, attn_ref, ga_ref, gyb_ref, wpa_ref, wout_ref, gain_ref, wg_ref, wu_ref, wd_ref,
                 o_ref, act_ref):
    y_a = _dot(attn_ref[...], wpa_ref[...])
    merged = ga_ref[...].astype(jnp.float32) * y_a + gyb_ref[...].astype(jnp.float32)
    x2 = x_ref[...] + _dot(_bf16(merged), wout_ref[...])
    o_ref[...] = _swiglu_half_step(x2, gain_ref[...], wg_ref, wu_ref, wd_ref, act_ref)


def _rows(tm, width):
    return pl.BlockSpec((tm, width), lambda i: (i, 0))


def _cols(height, tm):
    return pl.BlockSpec((height, tm), lambda i: (0, i))


def _resident():
    return pl.BlockSpec(memory_space=pltpu.VMEM)


def _params(n_axes):
    return pltpu.CompilerParams(dimension_semantics=("arbitrary",) * n_axes, vmem_limit_bytes=VMEM_LIMIT)


def _to_head_tiles(w, real):
    kdim = w.shape[0]
    w = w.reshape(kdim, -1, real)
    w = jnp.pad(w, ((0, 0), (0, 0), (0, 1)))
    src = np.where((HEAD_SRC >= 0) & (HEAD_SRC < real), HEAD_SRC, real)
    return w[:, :, src].reshape(kdim, -1)


def kernel(x, positions, ffn1_norm, ffn1_w_gate, ffn1_w_up, ffn1_w_down, mix_norm, w_in, gate_bias, q_a_norm, w_uq, kv_a_norm, w_uk, w_uv, q_head_norm, k_head_norm, w_proj_attn, conv_w, w_proj_conv, w_out, ffn2_norm, ffn2_w_gate, ffn2_w_up, ffn2_w_down):
    b, s, d = x.shape
    t = b * s
    assert d == D_MODEL and t % TM_FFN == 0 and s % TM_MIX == 0 and s % TQ == 0
    f32 = jnp.float32
    bf = jnp.bfloat16
    row = lambda g: g.reshape(1, -1).astype(f32)

    x_flat = x.reshape(t, d)
    pos_col = positions.reshape(t, 1).astype(f32)
    pos_row = positions.reshape(1, t).astype(f32)

    w_lat = _bf16(w_in[:, :LATENT_DIM])
    w_kr = jnp.pad(w_in[:, LATENT_DIM:REST_START], ((0, 0), (QK_NOPE_DIM, 0)))
    w_kr = _bf16(_to_head_tiles(w_kr, QK_HEAD_DIM))
    w_rest = _bf16(w_in[:, REST_START:])
    w_uq_t = _bf16(_to_head_tiles(w_uq, QK_HEAD_DIM).T)
    w_uk_pad = _bf16(_to_head_tiles(w_uk, QK_NOPE_DIM))
    v_pad = ((0, 0), (0, 0), (0, HEAD_PAD - V_HEAD_DIM))
    w_uv_t = _bf16(jnp.pad(w_uv.reshape(KV_LORA_RANK, N_HEADS, V_HEAD_DIM), v_pad).reshape(KV_LORA_RANK, -1).T)
    v_ones = jnp.zeros((N_HEADS, HEAD_PAD, 1), f32).at[:, V_HEAD_DIM].set(1.0).reshape(-1, 1)
    head_gain = lambda g: _to_head_tiles(g.reshape(1, QK_HEAD_DIM).astype(f32), QK_HEAD_DIM)
    q_gain_cols = jnp.broadcast_to(head_gain(q_head_norm).reshape(HEAD_PAD, 1), (HEAD_PAD, TM_MIX))
    inv_freq = 1.0 / (ROPE_THETA ** (jnp.arange(ROPE_HALF, dtype=f32) / ROPE_HALF))
    inv_freq_row = _to_head_tiles(
        jnp.concatenate([jnp.zeros((QK_NOPE_DIM,), f32), inv_freq, inv_freq]).reshape(1, QK_HEAD_DIM), QK_HEAD_DIM)
    inv_freq_cols = jnp.broadcast_to(inv_freq.reshape(ROPE_HALF, 1), (ROPE_HALF, TM_MIX))

    def ffn_weights(wg, wu, wd):
        return _bf16(wg), _bf16(wu), _bf16(wd)

    x1 = pl.pallas_call(
        _ffn_kernel,
        grid=(t // TM_FFN1,),
        in_specs=[_rows(TM_FFN1, d)] + [_resident()] * 4,
        out_specs=_rows(TM_FFN1, d),
        out_shape=jax.ShapeDtypeStruct((t, d), f32),
        scratch_shapes=[pltpu.VMEM((TM_FFN1, D_FF), bf)],
        compiler_params=_params(1),
        name="ffn1",
    )(x_flat, row(ffn1_norm), *ffn_weights(ffn1_w_gate, ffn1_w_up, ffn1_w_down))

    heads_w = N_HEADS * HEAD_PAD
    q_t, k, v_t, gate_a, gated_yb = pl.pallas_call(
        functools.partial(_mix_kernel, s),
        grid=(t // TM_MIX,),
        in_specs=[_rows(TM_MIX, d), _rows(TM_MIX, 1), _cols(1, TM_MIX)] + [_resident()] * 17,
        out_specs=[_cols(heads_w, TM_MIX), _rows(TM_MIX, heads_w), _cols(heads_w, TM_MIX),
                   _rows(TM_MIX, d), _rows(TM_MIX, d)],
        out_shape=[jax.ShapeDtypeStruct((heads_w, t), bf), jax.ShapeDtypeStruct((t, heads_w), bf),
                   jax.ShapeDtypeStruct((heads_w, t), bf), jax.ShapeDtypeStruct((t, d), bf),
                   jax.ShapeDtypeStruct((t, d), bf)],
        scratch_shapes=[pltpu.VMEM((TM_MIX + 2 * SUBLANES, CONV_DIM), f32)],
        compiler_params=_params(1),
        name="mix",
    )(x1, pos_col, pos_row, inv_freq_row, inv_freq_cols, row(mix_norm), w_lat, w_kr, w_rest, row(gate_bias), row(q_a_norm),
      w_uq_t, row(kv_a_norm), w_uk_pad, w_uv_t, v_ones, q_gain_cols, head_gain(k_head_norm),
      conv_w.astype(f32), _bf16(w_proj_conv))

    pair = 2 * HEAD_PAD
    feature_major = pl.BlockSpec((pair, s), lambda bi, hp: (hp, bi))
    attn = pl.pallas_call(
        _attn_kernel,
        grid=(b, N_HEADS // 2),
        in_specs=[feature_major, pl.BlockSpec((None, s, pair), lambda bi, hp: (bi, 0, hp)), feature_major],
        out_specs=pl.BlockSpec((None, s, LANES), lambda bi, hp: (bi, 0, hp)),
        out_shape=jax.ShapeDtypeStruct((b, s, N_HEADS * V_HEAD_DIM), bf),
        compiler_params=_params(2),
        name="attn",
    )(q_t, k.reshape(b, s, -1), v_t)

    out = pl.pallas_call(
        _post_kernel,
        grid=(t // TM_FFN,),
        in_specs=[_rows(TM_FFN, d), _rows(TM_FFN, N_HEADS * V_HEAD_DIM), _rows(TM_FFN, d), _rows(TM_FFN, d)]
        + [_resident()] * 6,
        out_specs=_rows(TM_FFN, d),
        out_shape=jax.ShapeDtypeStruct((t, d), f32),
        scratch_shapes=[pltpu.VMEM((TM_FFN, D_FF), bf)],
        compiler_params=_params(1),
        name="post",
    )(x1, attn.reshape(t, -1), gate_a, gated_yb, _bf16(w_proj_attn), _bf16(w_out), row(ffn2_norm),
      *ffn_weights(ffn2_w_gate, ffn2_w_up, ffn2_w_down))
    return out.reshape(b, s, d)
```

```python
import functools

import jax
import jax.numpy as jnp
import numpy as np
from jax import lax
from jax.experimental import pallas as pl
from jax.experimental.pallas import tpu as pltpu

D_MODEL = 1024
D_FF = 2816
N_HEADS = 8
QK_NOPE_DIM = 64
QK_ROPE_DIM = 32
QK_HEAD_DIM = QK_NOPE_DIM + QK_ROPE_DIM
V_HEAD_DIM = 64
Q_LORA_RANK = 384
KV_LORA_RANK = 256
CONV_DIM = 1024
CONV_WIDTH = 3
ROPE_THETA = 10000.0
NORM_EPS = 1e-6

LANES = 128
SUBLANES = 8
MXU_WIDTH = 256
HEAD_PAD = LANES
HALF_TILE = LANES // 2
ROPE_HALF = QK_ROPE_DIM // 2
ROPE_LO = HALF_TILE - ROPE_HALF


def _head_feature_sources():
    src = np.full((HEAD_PAD,), -1, np.int64)
    src[0:ROPE_LO] = np.arange(ROPE_LO)
    src[ROPE_LO:HALF_TILE] = QK_NOPE_DIM + np.arange(ROPE_HALF)
    rest = QK_NOPE_DIM - ROPE_LO
    src[HALF_TILE:HALF_TILE + rest] = ROPE_LO + np.arange(rest)
    src[HALF_TILE + ROPE_LO:HEAD_PAD] = QK_NOPE_DIM + ROPE_HALF + np.arange(ROPE_HALF)
    return src


HEAD_SRC = _head_feature_sources()

LATENT_DIM = Q_LORA_RANK + KV_LORA_RANK
REST_START = LATENT_DIM + QK_ROPE_DIM
OFF_XC = 0
OFF_GB = OFF_XC + CONV_DIM
OFF_GC = OFF_GB + CONV_DIM
OFF_GA_LOGIT = OFF_GC + CONV_DIM
OFF_GB_LOGIT = OFF_GA_LOGIT + D_MODEL

FF_CHUNKS = (768, 768, 768, 512)
VMEM_LIMIT = 56 * 1024 * 1024

TM_FFN = 512
TM_FFN1 = 1024
TM_MIX = 512
TQ = 256
PROJ_AHEAD = 1
SCORE_LOOKAHEAD = 3
MASK_VALUE = -1e30
LOG2_E = 1.4426950408889634


def _bf16(x):
    return x.astype(jnp.bfloat16)


def _dot(a, b):
    return jnp.dot(a, b, preferred_element_type=jnp.float32)


def _dot_nt(a, b):
    return lax.dot_general(a, b, (((1,), (1,)), ((), ())), preferred_element_type=jnp.float32)


def _sum_sq(x):
    return jnp.sum(x * x, axis=-1, keepdims=True)


def _rmsnorm(x, gain, n):
    return x * lax.rsqrt(_sum_sq(x) * (1.0 / n) + NORM_EPS) * gain


def _swiglu_half_step(x, gain, wg_ref, wu_ref, wd_ref, act_ref):
    h = _bf16(_rmsnorm(x, gain, D_MODEL))
    c0 = 0
    for ck in FF_CHUNKS:
        g = _dot(h, wg_ref[:, c0:c0 + ck])
        u = _dot(h, wu_ref[:, c0:c0 + ck])
        act_ref[:, c0:c0 + ck] = _bf16(g * jax.nn.sigmoid(g) * u)
        c0 += ck
    return x + 0.5 * _dot(act_ref[...], wd_ref[...])


def _ffn_kernel(x_ref, gain_ref, wg_ref, wu_ref, wd_ref, o_ref, act_ref):
    o_ref[...] = _swiglu_half_step(x_ref[...], gain_ref[...], wg_ref, wu_ref, wd_ref, act_ref)


def _swap_halves(t):
    return pltpu.roll(t, HALF_TILE, 1)


def _rope_tables_token_major(pos, inv_freq):
    half_rows = pos.shape[0] // 2
    lane = lax.broadcasted_iota(jnp.int32, (half_rows, LANES), 1)
    low = lane < HALF_TILE
    ang = jnp.where(low, pos[:half_rows], pos[half_rows:]) * inv_freq
    cos_p, sin_p = jnp.cos(ang), jnp.sin(ang)
    cos_s, sin_s = _swap_halves(cos_p), _swap_halves(sin_p)
    cos_t = jnp.concatenate([jnp.where(low, cos_p, cos_s), jnp.where(low, cos_s, cos_p)], axis=0)
    sin_t = jnp.concatenate([jnp.where(low, -sin_p, sin_s), jnp.where(low, -sin_s, sin_p)], axis=0)
    return cos_t, sin_t


def _q_head_feature_major(x, gain, cos, sin):
    inv = lax.rsqrt(jnp.sum(x * x, axis=0, keepdims=True) * (1.0 / QK_HEAD_DIM) + NORM_EPS)
    y = x * gain
    lo2 = HALF_TILE + ROPE_HALF
    y1, y2 = y[ROPE_LO:HALF_TILE], y[lo2:QK_HEAD_DIM]
    pad = jnp.zeros((HEAD_PAD - QK_HEAD_DIM, x.shape[1]), jnp.float32)
    return jnp.concatenate([y[0:ROPE_LO] * inv, (y1 * cos - y2 * sin) * inv, y[HALF_TILE:lo2] * inv, pad,
                            (y2 * cos + y1 * sin) * inv], axis=0)


def _mix_kernel(seq_len, x_ref, pos_col_ref, pos_row_ref, invf_row_ref, invf_col_ref, mixg_ref, wlat_t_ref,
                wrest_t_ref, bias_ref, qag_ref, wuqt_ref, kvag_ref, wuk_ref, wuvt_ref, qhg_ref, khg_ref,
                convw_ref, wpc_ref, qt_ref, k_ref, vt_ref, ga_ref, gyb_ref, ubuf_ref):
    tm = x_ref.shape[0]
    h = _bf16(_rmsnorm(x_ref[...], mixg_ref[...], D_MODEL))

    def proj(off, width):
        return _dot_nt(h, wrest_t_ref[off:off + width, :])

    @pl.when((pl.program_id(0) * tm) % seq_len == 0)
    def _():
        ubuf_ref[0:SUBLANES, :] = jnp.zeros((SUBLANES, CONV_DIM), jnp.float32)

    n_chunks = CONV_DIM // MXU_WIDTH
    heads_per_chunk = N_HEADS // n_chunks
    y_b = []
    gate_b = []

    def conv_projections(c):
        return tuple(proj(off + c * MXU_WIDTH, MXU_WIDTH) for off in (OFF_GC, OFF_XC, OFF_GB))

    def conv_chunk(c, projections):
        p_gc, p_xc, p_gb = projections
        cs = slice(c * MXU_WIDTH, (c + 1) * MXU_WIDTH)
        u = p_gc * p_xc
        ubuf_ref[SUBLANES:SUBLANES + tm, cs] = u
        z = (convw_ref[0:1, cs] * ubuf_ref[SUBLANES - 2:SUBLANES - 2 + tm, cs]
             + convw_ref[1:2, cs] * ubuf_ref[SUBLANES - 1:SUBLANES - 1 + tm, cs]
             + convw_ref[2:3, cs] * u)
        ubuf_ref[0:SUBLANES, cs] = ubuf_ref[tm:tm + SUBLANES, cs]
        part = _dot(_bf16(p_gb * z), wpc_ref[cs, :])
        y_b[:] = [part if not y_b else y_b[0] + part]

    def gate_projections(c):
        return tuple(proj(off + c * MXU_WIDTH, MXU_WIDTH) for off in (OFF_GA_LOGIT, OFF_GB_LOGIT))

    def gate_chunk(c, projections):
        p_ga, p_gbl = projections
        cs = slice(c * MXU_WIDTH, (c + 1) * MXU_WIDTH)
        ga_ref[:, cs] = _bf16(jax.nn.sigmoid(p_ga + bias_ref[:, cs]))
        gate_b.append(jax.nn.sigmoid(p_gbl + bias_ref[:, D_MODEL + cs.start:D_MODEL + cs.stop]))

    def q_heads(tokens, q_t):
        for hd in range(N_HEADS):
            qt_ref[hd * HEAD_PAD:(hd + 1) * HEAD_PAD, tokens] = _bf16(_q_head_feature_major(
                q_t[hd * QK_HEAD_DIM:(hd + 1) * QK_HEAD_DIM, :], q_gain[:, tokens], cos_q[:, tokens], sin_q[:, tokens]))

    def k_heads(c, k_nope):
        for i in range(heads_per_chunk):
            kn = k_nope[:, i * HEAD_PAD:(i + 1) * HEAD_PAD]
            inv = lax.rsqrt((_sum_sq(kn) + k_rope_sq) * (1.0 / QK_HEAD_DIM) + NORM_EPS)
            hd = c * heads_per_chunk + i
            k_ref[:, hd * HEAD_PAD:(hd + 1) * HEAD_PAD] = _bf16((kn * k_gain + k_rope_rot) * inv)

    q_kr = _dot_nt(h, wlat_t_ref[0:Q_LORA_RANK + HEAD_PAD, :])
    q_lat = q_kr[:, 0:Q_LORA_RANK]
    k_rope = q_kr[:, Q_LORA_RANK:]
    kv_lat = _dot_nt(h, wlat_t_ref[Q_LORA_RANK + HEAD_PAD:, :])
    conv_p = [conv_projections(c) for c in range(PROJ_AHEAD)]

    ang_q = invf_col_ref[...] * pos_row_ref[...]
    cos_q, sin_q = jnp.cos(ang_q), jnp.sin(ang_q)
    q_gain = qhg_ref[...] * (QK_HEAD_DIM ** -0.5 * LOG2_E)
    cos_k, sin_k = _rope_tables_token_major(pos_col_ref[...], invf_row_ref[...])
    k_gain = khg_ref[...]
    k_rope_g = k_rope * k_gain
    k_rope_rot = k_rope_g * cos_k + _swap_halves(k_rope_g) * sin_k
    k_rope_sq = _sum_sq(k_rope)

    qn = _bf16(_rmsnorm(q_lat, qag_ref[...], Q_LORA_RANK))
    ckv = _bf16(_rmsnorm(kv_lat, kvag_ref[...], KV_LORA_RANK))
    gate_p = []
    half = tm // 2
    for c in range(n_chunks):
        if c % 2 == 0:
            tokens = slice((c // 2) * half, (c // 2 + 1) * half)
            q_heads(tokens, _dot_nt(wuqt_ref[...], qn[tokens]))
        k_heads(c, _dot(ckv, wuk_ref[:, c * MXU_WIDTH:(c + 1) * MXU_WIDTH]))
        if c + PROJ_AHEAD < n_chunks:
            conv_p.append(conv_projections(c + PROJ_AHEAD))
        gate_p.append(gate_projections(c))
    v_t = _bf16(_dot_nt(wuvt_ref[...], ckv))
    pad_rows = lax.broadcasted_iota(jnp.int32, (HEAD_PAD - V_HEAD_DIM, tm), 0)
    ones_block = jnp.where(pad_rows == 0, 1.0, 0.0).astype(jnp.bfloat16)
    for hd in range(N_HEADS):
        vt_ref[hd * HEAD_PAD:hd * HEAD_PAD + V_HEAD_DIM, :] = v_t[hd * V_HEAD_DIM:(hd + 1) * V_HEAD_DIM, :]
        vt_ref[hd * HEAD_PAD + V_HEAD_DIM:(hd + 1) * HEAD_PAD, :] = ones_block

    for c in range(n_chunks):
        conv_chunk(c, conv_p[c])
        gate_chunk(c, gate_p[c])
    for c, g in enumerate(gate_b):
        cs = slice(c * MXU_WIDTH, (c + 1) * MXU_WIDTH)
        gyb_ref[:, cs] = _bf16(g * y_b[0][:, cs])


def _attn_kernel(qt_ref, k_ref, vt_ref, o_ref):
    s_len = k_ref.shape[0]
    key = lax.broadcasted_iota(jnp.int32, (TQ, TQ), 0)
    qry = lax.broadcasted_iota(jnp.int32, (TQ, TQ), 1)
    n_q = s_len // TQ

    def scores(qi, hd):
        lo, hi = qi * TQ, (qi + 1) * TQ
        hs = slice(hd * HEAD_PAD, (hd + 1) * HEAD_PAD)
        return _dot(k_ref[0:hi, hs], qt_ref[hs, lo:hi])

    def finish(qi, hd, s):
        lo, hi = qi * TQ, (qi + 1) * TQ
        hs = slice(hd * HEAD_PAD, (hd + 1) * HEAD_PAD)
        s_diag = jnp.where(key <= qry, s[lo:hi, :], MASK_VALUE)
        s = s_diag if qi == 0 else jnp.concatenate([s[0:lo, :], s_diag], axis=0)
        p = jnp.exp2(s - jnp.max(s, axis=0, keepdims=True))
        acc = _dot(vt_ref[hs, 0:hi], _bf16(p))
        return acc[0:V_HEAD_DIM, :] / acc[V_HEAD_DIM:V_HEAD_DIM + 1, :]

    chains = [(qi, hd) for qi in reversed(range(n_q)) for hd in range(2)]
    pending = [scores(*c) for c in chains[:SCORE_LOOKAHEAD]]
    outs = []
    for i, (qi, hd) in enumerate(chains):
        s = pending.pop(0)
        if i + SCORE_LOOKAHEAD < len(chains):
            pending.append(scores(*chains[i + SCORE_LOOKAHEAD]))
        outs.append(finish(qi, hd, s))
        if hd == 1:
            o_ref[qi * TQ:(qi + 1) * TQ, :] = _bf16(jnp.concatenate(outs, axis=0).T)
            outs = []


def _post_kernel(x_ref, attn_ref, ga_ref, gyb_ref, wpa_ref, wout_ref, gain_ref, wg_ref, wu_ref, wd_ref,
                 o_ref, act_ref):
    y_a = _dot(attn_ref[...], wpa_ref[...])
    merged = ga_ref[...].astype(jnp.float32) * y_a + gyb_ref[...].astype(jnp.float32)
    x2 = x_ref[...] + _dot(_bf16(merged), wout_ref[...])
    o_ref[...] = _swiglu_half_step(x2, gain_ref[...], wg_ref, wu_ref, wd_ref, act_ref)


def _rows(tm, width):
    return pl.BlockSpec((tm, width), lambda i: (i, 0))


def _cols(height, tm):
    return pl.BlockSpec((height, tm), lambda i: (0, i))


def _resident():
    return pl.BlockSpec(memory_space=pltpu.VMEM)


def _params(n_axes):
    return pltpu.CompilerParams(dimension_semantics=("arbitrary",) * n_axes, vmem_limit_bytes=VMEM_LIMIT)


def _to_head_tiles(w, real):
    kdim = w.shape[0]
    w = w.reshape(kdim, -1, real)
    w = jnp.pad(w, ((0, 0), (0, 0), (0, 1)))
    src = np.where((HEAD_SRC >= 0) & (HEAD_SRC < real), HEAD_SRC, real)
    return w[:, :, src].reshape(kdim, -1)


def kernel(x, positions, ffn1_norm, ffn1_w_gate, ffn1_w_up, ffn1_w_down, mix_norm, w_in, gate_bias, q_a_norm, w_uq, kv_a_norm, w_uk, w_uv, q_head_norm, k_head_norm, w_proj_attn, conv_w, w_proj_conv, w_out, ffn2_norm, ffn2_w_gate, ffn2_w_up, ffn2_w_down):
    b, s, d = x.shape
    t = b * s
    assert d == D_MODEL and t % TM_FFN == 0 and s % TM_MIX == 0 and s % TQ == 0
    f32 = jnp.float32
    bf = jnp.bfloat16
    row = lambda g: g.reshape(1, -1).astype(f32)

    x_flat = x.reshape(t, d)
    pos_col = positions.reshape(t, 1).astype(f32)
    pos_row = positions.reshape(1, t).astype(f32)

    w_in_t = _bf16(w_in).T
    w_kr_t = jnp.zeros((HEAD_PAD, D_MODEL), bf)
    w_kr_t = w_kr_t.at[ROPE_LO:HALF_TILE].set(w_in_t[LATENT_DIM:LATENT_DIM + ROPE_HALF])
    w_kr_t = w_kr_t.at[HALF_TILE + ROPE_LO:].set(w_in_t[LATENT_DIM + ROPE_HALF:REST_START])
    w_lat_t = jnp.concatenate([w_in_t[:Q_LORA_RANK], w_kr_t, w_in_t[Q_LORA_RANK:LATENT_DIM]], axis=0)
    w_rest_t = w_in_t[REST_START:]
    compact = HEAD_SRC[HEAD_SRC >= 0]
    w_uq_t = _bf16(w_uq.reshape(Q_LORA_RANK, N_HEADS, QK_HEAD_DIM)[:, :, compact].reshape(Q_LORA_RANK, -1).T)
    w_uk_pad = _bf16(_to_head_tiles(w_uk, QK_NOPE_DIM))
    w_uv_t = _bf16(w_uv.T)
    head_gain = lambda g: _to_head_tiles(g.reshape(1, QK_HEAD_DIM).astype(f32), QK_HEAD_DIM)
    q_gain_cols = jnp.broadcast_to(q_head_norm.astype(f32)[compact].reshape(QK_HEAD_DIM, 1), (QK_HEAD_DIM, TM_MIX))
    inv_freq = 1.0 / (ROPE_THETA ** (jnp.arange(ROPE_HALF, dtype=f32) / ROPE_HALF))
    inv_freq_row = _to_head_tiles(
        jnp.concatenate([jnp.zeros((QK_NOPE_DIM,), f32), inv_freq, inv_freq]).reshape(1, QK_HEAD_DIM), QK_HEAD_DIM)
    inv_freq_cols = jnp.broadcast_to(inv_freq.reshape(ROPE_HALF, 1), (ROPE_HALF, TM_MIX))

    def ffn_weights(wg, wu, wd):
        return _bf16(wg), _bf16(wu), _bf16(wd)

    x1 = pl.pallas_call(
        _ffn_kernel,
        grid=(t // TM_FFN1,),
        in_specs=[_rows(TM_FFN1, d)] + [_resident()] * 4,
        out_specs=_rows(TM_FFN1, d),
        out_shape=jax.ShapeDtypeStruct((t, d), f32),
        scratch_shapes=[pltpu.VMEM((TM_FFN1, D_FF), bf)],
        compiler_params=_params(1),
        name="ffn1",
    )(x_flat, row(ffn1_norm), *ffn_weights(ffn1_w_gate, ffn1_w_up, ffn1_w_down))

    heads_w = N_HEADS * HEAD_PAD
    q_t, k, v_t, gate_a, gated_yb = pl.pallas_call(
        functools.partial(_mix_kernel, s),
        grid=(t // TM_MIX,),
        in_specs=[_rows(TM_MIX, d), _rows(TM_MIX, 1), _cols(1, TM_MIX)] + [_resident()] * 15,
        out_specs=[_cols(heads_w, TM_MIX), _rows(TM_MIX, heads_w), _cols(heads_w, TM_MIX),
                   _rows(TM_MIX, d), _rows(TM_MIX, d)],
        out_shape=[jax.ShapeDtypeStruct((heads_w, t), bf), jax.ShapeDtypeStruct((t, heads_w), bf),
                   jax.ShapeDtypeStruct((heads_w, t), bf), jax.ShapeDtypeStruct((t, d), bf),
                   jax.ShapeDtypeStruct((t, d), bf)],
        scratch_shapes=[pltpu.VMEM((TM_MIX + 2 * SUBLANES, CONV_DIM), f32)],
        compiler_params=_params(1),
        name="mix",
    )(x1, pos_col, pos_row, inv_freq_row, inv_freq_cols, row(mix_norm), w_lat_t, w_rest_t, row(gate_bias), row(q_a_norm),
      w_uq_t, row(kv_a_norm), w_uk_pad, w_uv_t, q_gain_cols, head_gain(k_head_norm),
      conv_w.astype(f32), _bf16(w_proj_conv))

    pair = 2 * HEAD_PAD
    feature_major = pl.BlockSpec((pair, s), lambda bi, hp: (hp, bi))
    attn = pl.pallas_call(
        _attn_kernel,
        grid=(b, N_HEADS // 2),
        in_specs=[feature_major, pl.BlockSpec((None, s, pair), lambda bi, hp: (bi, 0, hp)), feature_major],
        out_specs=pl.BlockSpec((None, s, LANES), lambda bi, hp: (bi, 0, hp)),
        out_shape=jax.ShapeDtypeStruct((b, s, N_HEADS * V_HEAD_DIM), bf),
        compiler_params=_params(2),
        name="attn",
    )(q_t, k.reshape(b, s, -1), v_t)

    out = pl.pallas_call(
        _post_kernel,
        grid=(t // TM_FFN,),
        in_specs=[_rows(TM_FFN, d), _rows(TM_FFN, N_HEADS * V_HEAD_DIM), _rows(TM_FFN, d), _rows(TM_FFN, d)]
        + [_resident()] * 6,
        out_specs=_rows(TM_FFN, d),
        out_shape=jax.ShapeDtypeStruct((t, d), f32),
        scratch_shapes=[pltpu.VMEM((TM_FFN, D_FF), bf)],
        compiler_params=_params(1),
        name="post",
    )(x1, attn.reshape(t, -1), gate_a, gated_yb, _bf16(w_proj_attn), _bf16(w_out), row(ffn2_norm),
      *ffn_weights(ffn2_w_gate, ffn2_w_up, ffn2_w_down))
    return out.reshape(b, s, d)
```

```python
import functools

import jax
import jax.numpy as jnp
import numpy as np
from jax import lax
from jax.experimental import pallas as pl
from jax.experimental.pallas import tpu as pltpu

D_MODEL = 1024
D_FF = 2816
N_HEADS = 8
QK_NOPE_DIM = 64
QK_ROPE_DIM = 32
QK_HEAD_DIM = QK_NOPE_DIM + QK_ROPE_DIM
V_HEAD_DIM = 64
Q_LORA_RANK = 384
KV_LORA_RANK = 256
CONV_DIM = 1024
CONV_WIDTH = 3
ROPE_THETA = 10000.0
NORM_EPS = 1e-6

LANES = 128
SUBLANES = 8
MXU_WIDTH = 256
HEAD_PAD = LANES
HALF_TILE = LANES // 2
ROPE_HALF = QK_ROPE_DIM // 2
ROPE_LO = HALF_TILE - ROPE_HALF


def _head_feature_sources():
    src = np.full((HEAD_PAD,), -1, np.int64)
    src[0:ROPE_LO] = np.arange(ROPE_LO)
    src[ROPE_LO:HALF_TILE] = QK_NOPE_DIM + np.arange(ROPE_HALF)
    rest = QK_NOPE_DIM - ROPE_LO
    src[HALF_TILE:HALF_TILE + rest] = ROPE_LO + np.arange(rest)
    src[HALF_TILE + ROPE_LO:HEAD_PAD] = QK_NOPE_DIM + ROPE_HALF + np.arange(ROPE_HALF)
    return src


HEAD_SRC = _head_feature_sources()

LATENT_DIM = Q_LORA_RANK + KV_LORA_RANK
REST_START = LATENT_DIM + QK_ROPE_DIM
OFF_XC = 0
OFF_GB = OFF_XC + CONV_DIM
OFF_GC = OFF_GB + CONV_DIM
OFF_GA_LOGIT = OFF_GC + CONV_DIM
OFF_GB_LOGIT = OFF_GA_LOGIT + D_MODEL

FF_CHUNKS = (768, 768, 768, 512)
VMEM_LIMIT = 56 * 1024 * 1024

TM_FFN = 512
TM_FFN1 = 1024
TM_MIX = 512
TQ = 256
PROJ_AHEAD = 1
SCORE_LOOKAHEAD = 3
MASK_VALUE = -1e30
LOG2_E = 1.4426950408889634


def _bf16(x):
    return x.astype(jnp.bfloat16)


def _dot(a, b):
    return jnp.dot(a, b, preferred_element_type=jnp.float32)


def _dot_nt(a, b):
    return lax.dot_general(a, b, (((1,), (1,)), ((), ())), preferred_element_type=jnp.float32)


def _sum_sq(x):
    return jnp.sum(x * x, axis=-1, keepdims=True)


def _rmsnorm(x, gain, n):
    return x * lax.rsqrt(_sum_sq(x) * (1.0 / n) + NORM_EPS) * gain


def _swiglu_half_step(x, gain, wg_ref, wu_ref, wd_ref, act_ref):
    h = _bf16(_rmsnorm(x, gain, D_MODEL))
    c0 = 0
    for ck in FF_CHUNKS:
        g = _dot(h, wg_ref[:, c0:c0 + ck])
        u = _dot(h, wu_ref[:, c0:c0 + ck])
        act_ref[:, c0:c0 + ck] = _bf16(g * jax.nn.sigmoid(g) * u)
        c0 += ck
    return x + 0.5 * _dot(act_ref[...], wd_ref[...])


def _ffn_kernel(x_ref, gain_ref, wg_ref, wu_ref, wd_ref, o_ref, act_ref):
    o_ref[...] = _swiglu_half_step(x_ref[...], gain_ref[...], wg_ref, wu_ref, wd_ref, act_ref)


def _swap_halves(t):
    return pltpu.roll(t, HALF_TILE, 1)


def _rope_tables_token_major(pos, inv_freq):
    half_rows = pos.shape[0] // 2
    lane = lax.broadcasted_iota(jnp.int32, (half_rows, LANES), 1)
    low = lane < HALF_TILE
    ang = jnp.where(low, pos[:half_rows], pos[half_rows:]) * inv_freq
    cos_p, sin_p = jnp.cos(ang), jnp.sin(ang)
    cos_s, sin_s = _swap_halves(cos_p), _swap_halves(sin_p)
    cos_t = jnp.concatenate([jnp.where(low, cos_p, cos_s), jnp.where(low, cos_s, cos_p)], axis=0)
    sin_t = jnp.concatenate([jnp.where(low, -sin_p, sin_s), jnp.where(low, -sin_s, sin_p)], axis=0)
    return cos_t, sin_t


def _q_head_feature_major(x, gain, cos, sin):
    inv = lax.rsqrt(jnp.sum(x * x, axis=0, keepdims=True) * (1.0 / QK_HEAD_DIM) + NORM_EPS)
    y = x * gain
    lo2 = HALF_TILE + ROPE_HALF
    y1, y2 = y[ROPE_LO:HALF_TILE], y[lo2:QK_HEAD_DIM]
    pad = jnp.zeros((HEAD_PAD - QK_HEAD_DIM, x.shape[1]), jnp.float32)
    return jnp.concatenate([y[0:ROPE_LO] * inv, (y1 * cos - y2 * sin) * inv, y[HALF_TILE:lo2] * inv, pad,
                            (y2 * cos + y1 * sin) * inv], axis=0)


def _mix_kernel(seq_len, x_ref, pos_col_ref, pos_row_ref, invf_row_ref, invf_col_ref, mixg_ref, wlat_t_ref,
                wrest_t_ref, bias_ref, qag_ref, wuqt_ref, kvag_ref, wuk_ref, wuvt_ref, qhg_ref, khg_ref,
                convw_ref, wpc_ref, qt_ref, k_ref, vt_ref, ga_ref, gyb_ref, ubuf_ref):
    tm = x_ref.shape[0]
    h = _bf16(_rmsnorm(x_ref[...], mixg_ref[...], D_MODEL))

    def proj(off, width):
        return _dot_nt(h, wrest_t_ref[off:off + width, :])

    @pl.when((pl.program_id(0) * tm) % seq_len == 0)
    def _():
        ubuf_ref[0:SUBLANES, :] = jnp.zeros((SUBLANES, CONV_DIM), jnp.float32)

    n_chunks = CONV_DIM // MXU_WIDTH
    heads_per_chunk = N_HEADS // n_chunks
    y_b = []
    gate_b = []

    def conv_projections(c):
        return tuple(proj(off + c * MXU_WIDTH, MXU_WIDTH) for off in (OFF_GC, OFF_XC, OFF_GB))

    def conv_chunk(c, projections):
        p_gc, p_xc, p_gb = projections
        cs = slice(c * MXU_WIDTH, (c + 1) * MXU_WIDTH)
        u = p_gc * p_xc
        ubuf_ref[SUBLANES:SUBLANES + tm, cs] = u
        z = (convw_ref[0:1, cs] * ubuf_ref[SUBLANES - 2:SUBLANES - 2 + tm, cs]
             + convw_ref[1:2, cs] * ubuf_ref[SUBLANES - 1:SUBLANES - 1 + tm, cs]
             + convw_ref[2:3, cs] * u)
        ubuf_ref[0:SUBLANES, cs] = ubuf_ref[tm:tm + SUBLANES, cs]
        part = _dot(_bf16(p_gb * z), wpc_ref[cs, :])
        y_b[:] = [part if not y_b else y_b[0] + part]

    def gate_projections(c):
        return tuple(proj(off + c * MXU_WIDTH, MXU_WIDTH) for off in (OFF_GA_LOGIT, OFF_GB_LOGIT))

    def gate_chunk(c, projections):
        p_ga, p_gbl = projections
        cs = slice(c * MXU_WIDTH, (c + 1) * MXU_WIDTH)
        ga_ref[:, cs] = _bf16(jax.nn.sigmoid(p_ga + bias_ref[:, cs]))
        gate_b.append(jax.nn.sigmoid(p_gbl + bias_ref[:, D_MODEL + cs.start:D_MODEL + cs.stop]))

    def q_heads(tokens, q_t):
        for hd in range(N_HEADS):
            qt_ref[hd * HEAD_PAD:(hd + 1) * HEAD_PAD, tokens] = _bf16(_q_head_feature_major(
                q_t[hd * QK_HEAD_DIM:(hd + 1) * QK_HEAD_DIM, :], q_gain[:, tokens], cos_q[:, tokens], sin_q[:, tokens]))

    def k_heads(c, k_nope):
        for i in range(heads_per_chunk):
            kn = k_nope[:, i * HEAD_PAD:(i + 1) * HEAD_PAD]
            inv = lax.rsqrt((_sum_sq(kn) + k_rope_sq) * (1.0 / QK_HEAD_DIM) + NORM_EPS)
            k_ref[c, :, i * HEAD_PAD:(i + 1) * HEAD_PAD] = _bf16((kn * k_gain + k_rope_rot) * inv)

    q_kr = _dot_nt(h, wlat_t_ref[0:Q_LORA_RANK + HEAD_PAD, :])
    q_lat = q_kr[:, 0:Q_LORA_RANK]
    k_rope = q_kr[:, Q_LORA_RANK:]
    kv_lat = _dot_nt(h, wlat_t_ref[Q_LORA_RANK + HEAD_PAD:, :])
    conv_p = [conv_projections(c) for c in range(PROJ_AHEAD)]

    ang_q = invf_col_ref[...] * pos_row_ref[...]
    cos_q, sin_q = jnp.cos(ang_q), jnp.sin(ang_q)
    q_gain = qhg_ref[...] * (QK_HEAD_DIM ** -0.5 * LOG2_E)
    cos_k, sin_k = _rope_tables_token_major(pos_col_ref[...], invf_row_ref[...])
    k_gain = khg_ref[...]
    k_rope_g = k_rope * k_gain
    k_rope_rot = k_rope_g * cos_k + _swap_halves(k_rope_g) * sin_k
    k_rope_sq = _sum_sq(k_rope)

    qn = _bf16(_rmsnorm(q_lat, qag_ref[...], Q_LORA_RANK))
    ckv = _bf16(_rmsnorm(kv_lat, kvag_ref[...], KV_LORA_RANK))
    gate_p = []
    half = tm // 2
    for c in range(n_chunks):
        if c % 2 == 0:
            tokens = slice((c // 2) * half, (c // 2 + 1) * half)
            q_heads(tokens, _dot_nt(wuqt_ref[...], qn[tokens]))
        k_heads(c, _dot(ckv, wuk_ref[:, c * MXU_WIDTH:(c + 1) * MXU_WIDTH]))
        if c + PROJ_AHEAD < n_chunks:
            conv_p.append(conv_projections(c + PROJ_AHEAD))
        gate_p.append(gate_projections(c))
    v_t = _bf16(_dot_nt(wuvt_ref[...], ckv))
    pad_rows = lax.broadcasted_iota(jnp.int32, (HEAD_PAD - V_HEAD_DIM, tm), 0)
    ones_block = jnp.where(pad_rows == 0, 1.0, 0.0).astype(jnp.bfloat16)
    for hd in range(N_HEADS):
        vt_ref[hd * HEAD_PAD:hd * HEAD_PAD + V_HEAD_DIM, :] = v_t[hd * V_HEAD_DIM:(hd + 1) * V_HEAD_DIM, :]
        vt_ref[hd * HEAD_PAD + V_HEAD_DIM:(hd + 1) * HEAD_PAD, :] = ones_block

    for c in range(n_chunks):
        conv_chunk(c, conv_p[c])
        gate_chunk(c, gate_p[c])
    for c, g in enumerate(gate_b):
        cs = slice(c * MXU_WIDTH, (c + 1) * MXU_WIDTH)
        gyb_ref[:, cs] = _bf16(g * y_b[0][:, cs])


def _attn_kernel(qt_ref, k_ref, vt_ref, o_ref):
    s_len = k_ref.shape[0]
    key = lax.broadcasted_iota(jnp.int32, (TQ, TQ), 0)
    qry = lax.broadcasted_iota(jnp.int32, (TQ, TQ), 1)
    n_q = s_len // TQ
    tm = qt_ref.shape[2]

    def tokens(ref, hs, lo, hi):
        parts = [ref[j, hs, max(lo - j * tm, 0):min(hi - j * tm, tm)] for j in range(lo // tm, -(-hi // tm))]
        return parts[0] if len(parts) == 1 else jnp.concatenate(parts, axis=1)

    def scores(qi, hd):
        lo, hi = qi * TQ, (qi + 1) * TQ
        hs = slice(hd * HEAD_PAD, (hd + 1) * HEAD_PAD)
        return _dot(k_ref[0:hi, hs], tokens(qt_ref, hs, lo, hi))

    def finish(qi, hd, s):
        lo, hi = qi * TQ, (qi + 1) * TQ
        hs = slice(hd * HEAD_PAD, (hd + 1) * HEAD_PAD)
        s_diag = jnp.where(key <= qry, s[lo:hi, :], MASK_VALUE)
        s = s_diag if qi == 0 else jnp.concatenate([s[0:lo, :], s_diag], axis=0)
        p = jnp.exp2(s - jnp.max(s, axis=0, keepdims=True))
        acc = _dot(tokens(vt_ref, hs, 0, hi), _bf16(p))
        return acc[0:V_HEAD_DIM, :] / acc[V_HEAD_DIM:V_HEAD_DIM + 1, :]

    chains = [(qi, hd) for qi in reversed(range(n_q)) for hd in range(2)]
    pending = [scores(*c) for c in chains[:SCORE_LOOKAHEAD]]
    outs = []
    for i, (qi, hd) in enumerate(chains):
        s = pending.pop(0)
        if i + SCORE_LOOKAHEAD < len(chains):
            pending.append(scores(*chains[i + SCORE_LOOKAHEAD]))
        outs.append(finish(qi, hd, s))
        if hd == 1:
            o_ref[qi * TQ:(qi + 1) * TQ, :] = _bf16(jnp.concatenate(outs, axis=0).T)
            outs = []


def _post_kernel(x_ref, attn_ref, ga_ref, gyb_ref, wpa_ref, wout_ref, gain_ref, wg_ref, wu_ref, wd_ref,
                 o_ref, act_ref):
    attn = jnp.concatenate([attn_ref[hp] for hp in range(attn_ref.shape[0])], axis=1)
    y_a = _dot(attn, wpa_ref[...])
    merged = ga_ref[...].astype(jnp.float32) * y_a + gyb_ref[...].astype(jnp.float32)
    x2 = x_ref[...] + _dot(_bf16(merged), wout_ref[...])
    o_ref[...] = _swiglu_half_step(x2, gain_ref[...], wg_ref, wu_ref, wd_ref, act_ref)


def _rows(tm, width):
    return pl.BlockSpec((tm, width), lambda i: (i, 0))


def _cols(height, tm):
    return pl.BlockSpec((height, tm), lambda i: (0, i))


def _resident():
    return pl.BlockSpec(memory_space=pltpu.VMEM)


def _params(n_axes):
    return pltpu.CompilerParams(dimension_semantics=("arbitrary",) * n_axes, vmem_limit_bytes=VMEM_LIMIT)


def _to_head_tiles(w, real):
    kdim = w.shape[0]
    w = w.reshape(kdim, -1, real)
    w = jnp.pad(w, ((0, 0), (0, 0), (0, 1)))
    src = np.where((HEAD_SRC >= 0) & (HEAD_SRC < real), HEAD_SRC, real)
    return w[:, :, src].reshape(kdim, -1)


def kernel(x, positions, ffn1_norm, ffn1_w_gate, ffn1_w_up, ffn1_w_down, mix_norm, w_in, gate_bias, q_a_norm, w_uq, kv_a_norm, w_uk, w_uv, q_head_norm, k_head_norm, w_proj_attn, conv_w, w_proj_conv, w_out, ffn2_norm, ffn2_w_gate, ffn2_w_up, ffn2_w_down):
    b, s, d = x.shape
    t = b * s
    assert d == D_MODEL and t % TM_FFN == 0 and s % TM_MIX == 0 and s % TQ == 0
    f32 = jnp.float32
    bf = jnp.bfloat16
    row = lambda g: g.reshape(1, -1).astype(f32)

    x_flat = x.reshape(t, d)
    pos_col = positions.reshape(t, 1).astype(f32)
    pos_row = positions.reshape(1, t).astype(f32)

    w_in_t = _bf16(w_in).T
    w_kr_t = jnp.zeros((HEAD_PAD, D_MODEL), bf)
    w_kr_t = w_kr_t.at[ROPE_LO:HALF_TILE].set(w_in_t[LATENT_DIM:LATENT_DIM + ROPE_HALF])
    w_kr_t = w_kr_t.at[HALF_TILE + ROPE_LO:].set(w_in_t[LATENT_DIM + ROPE_HALF:REST_START])
    w_lat_t = jnp.concatenate([w_in_t[:Q_LORA_RANK], w_kr_t, w_in_t[Q_LORA_RANK:LATENT_DIM]], axis=0)
    w_rest_t = w_in_t[REST_START:]
    compact = HEAD_SRC[HEAD_SRC >= 0]
    w_uq_t = _bf16(w_uq.reshape(Q_LORA_RANK, N_HEADS, QK_HEAD_DIM)[:, :, compact].reshape(Q_LORA_RANK, -1).T)
    w_uk_pad = _bf16(_to_head_tiles(w_uk, QK_NOPE_DIM))
    w_uv_t = _bf16(w_uv.T)
    head_gain = lambda g: _to_head_tiles(g.reshape(1, QK_HEAD_DIM).astype(f32), QK_HEAD_DIM)
    q_gain_cols = jnp.broadcast_to(q_head_norm.astype(f32)[compact].reshape(QK_HEAD_DIM, 1), (QK_HEAD_DIM, TM_MIX))
    inv_freq = 1.0 / (ROPE_THETA ** (jnp.arange(ROPE_HALF, dtype=f32) / ROPE_HALF))
    inv_freq_row = _to_head_tiles(
        jnp.concatenate([jnp.zeros((QK_NOPE_DIM,), f32), inv_freq, inv_freq]).reshape(1, QK_HEAD_DIM), QK_HEAD_DIM)
    inv_freq_cols = jnp.broadcast_to(inv_freq.reshape(ROPE_HALF, 1), (ROPE_HALF, TM_MIX))

    def ffn_weights(wg, wu, wd):
        return _bf16(wg), _bf16(wu), _bf16(wd)

    x1 = pl.pallas_call(
        _ffn_kernel,
        grid=(t // TM_FFN1,),
        in_specs=[_rows(TM_FFN1, d)] + [_resident()] * 4,
        out_specs=_rows(TM_FFN1, d),
        out_shape=jax.ShapeDtypeStruct((t, d), f32),
        scratch_shapes=[pltpu.VMEM((TM_FFN1, D_FF), bf)],
        compiler_params=_params(1),
        name="ffn1",
    )(x_flat, row(ffn1_norm), *ffn_weights(ffn1_w_gate, ffn1_w_up, ffn1_w_down))

    heads_w = N_HEADS * HEAD_PAD
    pair = 2 * HEAD_PAD
    n_pairs = N_HEADS // 2
    tile_major = pl.BlockSpec((None, heads_w, TM_MIX), lambda i: (i, 0, 0))
    q_t, k, v_t, gate_a, gated_yb = pl.pallas_call(
        functools.partial(_mix_kernel, s),
        grid=(t // TM_MIX,),
        in_specs=[_rows(TM_MIX, d), _rows(TM_MIX, 1), _cols(1, TM_MIX)] + [_resident()] * 15,
        out_specs=[tile_major, pl.BlockSpec((n_pairs, TM_MIX, pair), lambda i: (0, i, 0)), tile_major,
                   _rows(TM_MIX, d), _rows(TM_MIX, d)],
        out_shape=[jax.ShapeDtypeStruct((t // TM_MIX, heads_w, TM_MIX), bf), jax.ShapeDtypeStruct((n_pairs, t, pair), bf),
                   jax.ShapeDtypeStruct((t // TM_MIX, heads_w, TM_MIX), bf), jax.ShapeDtypeStruct((t, d), bf),
                   jax.ShapeDtypeStruct((t, d), bf)],
        scratch_shapes=[pltpu.VMEM((TM_MIX + 2 * SUBLANES, CONV_DIM), f32)],
        compiler_params=_params(1),
        name="mix",
    )(x1, pos_col, pos_row, inv_freq_row, inv_freq_cols, row(mix_norm), w_lat_t, w_rest_t, row(gate_bias), row(q_a_norm),
      w_uq_t, row(kv_a_norm), w_uk_pad, w_uv_t, q_gain_cols, head_gain(k_head_norm),
      conv_w.astype(f32), _bf16(w_proj_conv))

    feature_major = pl.BlockSpec((s // TM_MIX, pair, TM_MIX), lambda bi, hp: (bi, hp, 0))
    attn = pl.pallas_call(
        _attn_kernel,
        grid=(b, n_pairs),
        in_specs=[feature_major, pl.BlockSpec((None, s, pair), lambda bi, hp: (hp, bi, 0)), feature_major],
        out_specs=pl.BlockSpec((None, s, LANES), lambda bi, hp: (hp, bi, 0)),
        out_shape=jax.ShapeDtypeStruct((n_pairs, t, LANES), bf),
        compiler_params=_params(2),
        name="attn",
    )(q_t, k, v_t)

    out = pl.pallas_call(
        _post_kernel,
        grid=(t // TM_FFN,),
        in_specs=[_rows(TM_FFN, d), pl.BlockSpec((n_pairs, TM_FFN, LANES), lambda i: (0, i, 0)),
                  _rows(TM_FFN, d), _rows(TM_FFN, d)]
        + [_resident()] * 6,
        out_specs=_rows(TM_FFN, d),
        out_shape=jax.ShapeDtypeStruct((t, d), f32),
        scratch_shapes=[pltpu.VMEM((TM_FFN, D_FF), bf)],
        compiler_params=_params(1),
        name="post",
    )(x1, attn, gate_a, gated_yb, _bf16(w_proj_attn), _bf16(w_out), row(ffn2_norm),
      *ffn_weights(ffn2_w_gate, ffn2_w_up, ffn2_w_down))
    return out.reshape(b, s, d)
```

```python
import functools

import jax
import jax.numpy as jnp
import numpy as np
from jax import lax
from jax.experimental import pallas as pl
from jax.experimental.pallas import tpu as pltpu

D_MODEL = 1024
D_FF = 2816
N_HEADS = 8
QK_NOPE_DIM = 64
QK_ROPE_DIM = 32
QK_HEAD_DIM = QK_NOPE_DIM + QK_ROPE_DIM
V_HEAD_DIM = 64
Q_LORA_RANK = 384
KV_LORA_RANK = 256
CONV_DIM = 1024
CONV_WIDTH = 3
ROPE_THETA = 10000.0
NORM_EPS = 1e-6

LANES = 128
SUBLANES = 8
MXU_WIDTH = 256
HEAD_PAD = LANES
HALF_TILE = LANES // 2
ROPE_HALF = QK_ROPE_DIM // 2
ROPE_LO = HALF_TILE - ROPE_HALF


def _head_feature_sources():
    src = np.full((HEAD_PAD,), -1, np.int64)
    src[0:ROPE_LO] = np.arange(ROPE_LO)
    src[ROPE_LO:HALF_TILE] = QK_NOPE_DIM + np.arange(ROPE_HALF)
    rest = QK_NOPE_DIM - ROPE_LO
    src[HALF_TILE:HALF_TILE + rest] = ROPE_LO + np.arange(rest)
    src[HALF_TILE + ROPE_LO:HEAD_PAD] = QK_NOPE_DIM + ROPE_HALF + np.arange(ROPE_HALF)
    return src


HEAD_SRC = _head_feature_sources()

LATENT_DIM = Q_LORA_RANK + KV_LORA_RANK
REST_START = LATENT_DIM + QK_ROPE_DIM
OFF_XC = 0
OFF_GB = OFF_XC + CONV_DIM
OFF_GC = OFF_GB + CONV_DIM
OFF_GA_LOGIT = OFF_GC + CONV_DIM
OFF_GB_LOGIT = OFF_GA_LOGIT + D_MODEL

FF_CHUNKS = (768, 768, 768, 512)
VMEM_LIMIT = 56 * 1024 * 1024

TM_FFN = 512
TM_FFN1 = 1024
TM_MIX = 512
TQ = 256
ATTN_PAIRS_PER_STEP = 4
PROJ_AHEAD = 1
SCORE_LOOKAHEAD = 3
MASK_VALUE = -1e30
LOG2_E = 1.4426950408889634


def _bf16(x):
    return x.astype(jnp.bfloat16)


def _dot(a, b):
    return jnp.dot(a, b, preferred_element_type=jnp.float32)


def _dot_nt(a, b):
    return lax.dot_general(a, b, (((1,), (1,)), ((), ())), preferred_element_type=jnp.float32)


def _sum_sq(x):
    return jnp.sum(x * x, axis=-1, keepdims=True)


def _rmsnorm(x, gain, n):
    return x * lax.rsqrt(_sum_sq(x) * (1.0 / n) + NORM_EPS) * gain


def _swiglu_half_step(x, gain, wg_ref, wu_ref, wd_ref, act_ref):
    h = _bf16(_rmsnorm(x, gain, D_MODEL))
    c0 = 0
    for ck in FF_CHUNKS:
        g = _dot(h, wg_ref[:, c0:c0 + ck])
        u = _dot(h, wu_ref[:, c0:c0 + ck])
        act_ref[:, c0:c0 + ck] = _bf16(g * jax.nn.sigmoid(g) * u)
        c0 += ck
    return x + 0.5 * _dot(act_ref[...], wd_ref[...])


def _ffn_kernel(x_ref, gain_ref, wg_ref, wu_ref, wd_ref, o_ref, act_ref):
    o_ref[...] = _swiglu_half_step(x_ref[...], gain_ref[...], wg_ref, wu_ref, wd_ref, act_ref)


def _swap_halves(t):
    return pltpu.roll(t, HALF_TILE, 1)


def _rope_tables_token_major(pos, inv_freq):
    half_rows = pos.shape[0] // 2
    lane = lax.broadcasted_iota(jnp.int32, (half_rows, LANES), 1)
    low = lane < HALF_TILE
    ang = jnp.where(low, pos[:half_rows], pos[half_rows:]) * inv_freq
    cos_p, sin_p = jnp.cos(ang), jnp.sin(ang)
    cos_s, sin_s = _swap_halves(cos_p), _swap_halves(sin_p)
    cos_t = jnp.concatenate([jnp.where(low, cos_p, cos_s), jnp.where(low, cos_s, cos_p)], axis=0)
    sin_t = jnp.concatenate([jnp.where(low, -sin_p, sin_s), jnp.where(low, -sin_s, sin_p)], axis=0)
    return cos_t, sin_t


def _q_head_feature_major(x, gain, cos, sin):
    inv = lax.rsqrt(jnp.sum(x * x, axis=0, keepdims=True) * (1.0 / QK_HEAD_DIM) + NORM_EPS)
    y = x * gain
    lo2 = HALF_TILE + ROPE_HALF
    y1, y2 = y[ROPE_LO:HALF_TILE], y[lo2:QK_HEAD_DIM]
    pad = jnp.zeros((HEAD_PAD - QK_HEAD_DIM, x.shape[1]), jnp.float32)
    return jnp.concatenate([y[0:ROPE_LO] * inv, (y1 * cos - y2 * sin) * inv, y[HALF_TILE:lo2] * inv, pad,
                            (y2 * cos + y1 * sin) * inv], axis=0)


def _mix_kernel(seq_len, x_ref, pos_col_ref, pos_row_ref, invf_row_ref, invf_col_ref, mixg_ref, wlat_t_ref,
                wrest_t_ref, bias_ref, qag_ref, wuqt_ref, kvag_ref, wuk_ref, wuvt_ref, qhg_ref, khg_ref,
                convw_ref, wpc_ref, qt_ref, k_ref, vt_ref, ga_ref, gyb_ref, ubuf_ref):
    tm = x_ref.shape[0]
    h = _bf16(_rmsnorm(x_ref[...], mixg_ref[...], D_MODEL))

    def proj(off, width):
        return _dot_nt(h, wrest_t_ref[off:off + width, :])

    @pl.when((pl.program_id(0) * tm) % seq_len == 0)
    def _():
        ubuf_ref[0:SUBLANES, :] = jnp.zeros((SUBLANES, CONV_DIM), jnp.float32)

    n_chunks = CONV_DIM // MXU_WIDTH
    heads_per_chunk = N_HEADS // n_chunks
    y_b = []
    gate_b = []

    def conv_projections(c):
        return tuple(proj(off + c * MXU_WIDTH, MXU_WIDTH) for off in (OFF_GC, OFF_XC, OFF_GB))

    def conv_chunk(c, projections):
        p_gc, p_xc, p_gb = projections
        cs = slice(c * MXU_WIDTH, (c + 1) * MXU_WIDTH)
        u = p_gc * p_xc
        ubuf_ref[SUBLANES:SUBLANES + tm, cs] = u
        z = (convw_ref[0:1, cs] * ubuf_ref[SUBLANES - 2:SUBLANES - 2 + tm, cs]
             + convw_ref[1:2, cs] * ubuf_ref[SUBLANES - 1:SUBLANES - 1 + tm, cs]
             + convw_ref[2:3, cs] * u)
        ubuf_ref[0:SUBLANES, cs] = ubuf_ref[tm:tm + SUBLANES, cs]
        part = _dot(_bf16(p_gb * z), wpc_ref[cs, :])
        y_b[:] = [part if not y_b else y_b[0] + part]

    def gate_projections(c):
        return tuple(proj(off + c * MXU_WIDTH, MXU_WIDTH) for off in (OFF_GA_LOGIT, OFF_GB_LOGIT))

    def gate_chunk(c, projections):
        p_ga, p_gbl = projections
        cs = slice(c * MXU_WIDTH, (c + 1) * MXU_WIDTH)
        ga_ref[:, cs] = _bf16(jax.nn.sigmoid(p_ga + bias_ref[:, cs]))
        gate_b.append(jax.nn.sigmoid(p_gbl + bias_ref[:, D_MODEL + cs.start:D_MODEL + cs.stop]))

    def q_heads(tokens, q_t):
        for hd in range(N_HEADS):
            qt_ref[hd * HEAD_PAD:(hd + 1) * HEAD_PAD, tokens] = _bf16(_q_head_feature_major(
                q_t[hd * QK_HEAD_DIM:(hd + 1) * QK_HEAD_DIM, :], q_gain[:, tokens], cos_q[:, tokens], sin_q[:, tokens]))

    def k_heads(c, k_nope):
        for i in range(heads_per_chunk):
            kn = k_nope[:, i * HEAD_PAD:(i + 1) * HEAD_PAD]
            inv = lax.rsqrt((_sum_sq(kn) + k_rope_sq) * (1.0 / QK_HEAD_DIM) + NORM_EPS)
            k_ref[c, :, i * HEAD_PAD:(i + 1) * HEAD_PAD] = _bf16((kn * k_gain + k_rope_rot) * inv)

    q_kr = _dot_nt(h, wlat_t_ref[0:Q_LORA_RANK + HEAD_PAD, :])
    q_lat = q_kr[:, 0:Q_LORA_RANK]
    k_rope = q_kr[:, Q_LORA_RANK:]
    kv_lat = _dot_nt(h, wlat_t_ref[Q_LORA_RANK + HEAD_PAD:, :])
    conv_p = [conv_projections(c) for c in range(PROJ_AHEAD)]

    ang_q = invf_col_ref[...] * pos_row_ref[...]
    cos_q, sin_q = jnp.cos(ang_q), jnp.sin(ang_q)
    q_gain = qhg_ref[...] * (QK_HEAD_DIM ** -0.5 * LOG2_E)
    cos_k, sin_k = _rope_tables_token_major(pos_col_ref[...], invf_row_ref[...])
    k_gain = khg_ref[...]
    k_rope_g = k_rope * k_gain
    k_rope_rot = k_rope_g * cos_k + _swap_halves(k_rope_g) * sin_k
    k_rope_sq = _sum_sq(k_rope)

    qn = _bf16(_rmsnorm(q_lat, qag_ref[...], Q_LORA_RANK))
    ckv = _bf16(_rmsnorm(kv_lat, kvag_ref[...], KV_LORA_RANK))
    gate_p = []
    half = tm // 2
    for c in range(n_chunks):
        if c % 2 == 0:
            tokens = slice((c // 2) * half, (c // 2 + 1) * half)
            q_heads(tokens, _dot_nt(wuqt_ref[...], qn[tokens]))
        k_heads(c, _dot(ckv, wuk_ref[:, c * MXU_WIDTH:(c + 1) * MXU_WIDTH]))
        if c + PROJ_AHEAD < n_chunks:
            conv_p.append(conv_projections(c + PROJ_AHEAD))
        gate_p.append(gate_projections(c))
    v_t = _bf16(_dot_nt(wuvt_ref[...], ckv))
    pad_rows = lax.broadcasted_iota(jnp.int32, (HEAD_PAD - V_HEAD_DIM, tm), 0)
    ones_block = jnp.where(pad_rows == 0, 1.0, 0.0).astype(jnp.bfloat16)
    for hd in range(N_HEADS):
        vt_ref[hd * HEAD_PAD:hd * HEAD_PAD + V_HEAD_DIM, :] = v_t[hd * V_HEAD_DIM:(hd + 1) * V_HEAD_DIM, :]
        vt_ref[hd * HEAD_PAD + V_HEAD_DIM:(hd + 1) * HEAD_PAD, :] = ones_block

    for c in range(n_chunks):
        conv_chunk(c, conv_p[c])
        gate_chunk(c, gate_p[c])
    for c, g in enumerate(gate_b):
        cs = slice(c * MXU_WIDTH, (c + 1) * MXU_WIDTH)
        gyb_ref[:, cs] = _bf16(g * y_b[0][:, cs])


def _attn_kernel(qt_ref, k_ref, vt_ref, o_ref):
    n_pairs, s_len, _ = k_ref.shape
    key = lax.broadcasted_iota(jnp.int32, (TQ, TQ), 0)
    qry = lax.broadcasted_iota(jnp.int32, (TQ, TQ), 1)
    n_q = s_len // TQ
    tm = qt_ref.shape[2]

    def tokens(ref, hd, lo, hi):
        hs = slice(hd * HEAD_PAD, (hd + 1) * HEAD_PAD)
        parts = [ref[j, hs, max(lo - j * tm, 0):min(hi - j * tm, tm)] for j in range(lo // tm, -(-hi // tm))]
        return parts[0] if len(parts) == 1 else jnp.concatenate(parts, axis=1)

    def scores(qi, hd):
        lo, hi = qi * TQ, (qi + 1) * TQ
        k = k_ref[hd // 2, 0:hi, (hd % 2) * HEAD_PAD:(hd % 2 + 1) * HEAD_PAD]
        return _dot(k, tokens(qt_ref, hd, lo, hi))

    def finish(qi, hd, s):
        lo, hi = qi * TQ, (qi + 1) * TQ
        s_diag = jnp.where(key <= qry, s[lo:hi, :], MASK_VALUE)
        s = s_diag if qi == 0 else jnp.concatenate([s[0:lo, :], s_diag], axis=0)
        p = jnp.exp2(s - jnp.max(s, axis=0, keepdims=True))
        acc = _dot(tokens(vt_ref, hd, 0, hi), _bf16(p))
        return acc[0:V_HEAD_DIM, :] / acc[V_HEAD_DIM:V_HEAD_DIM + 1, :]

    chains = [(qi, hd) for qi in reversed(range(n_q)) for hd in range(2 * n_pairs)]
    pending = [scores(*c) for c in chains[:SCORE_LOOKAHEAD]]
    outs = []
    for i, (qi, hd) in enumerate(chains):
        s = pending.pop(0)
        if i + SCORE_LOOKAHEAD < len(chains):
            pending.append(scores(*chains[i + SCORE_LOOKAHEAD]))
        outs.append(finish(qi, hd, s))
        if hd % 2 == 1:
            o_ref[hd // 2, qi * TQ:(qi + 1) * TQ, :] = _bf16(jnp.concatenate(outs, axis=0).T)
            outs = []


def _post_kernel(x_ref, attn_ref, ga_ref, gyb_ref, wpa_ref, wout_ref, gain_ref, wg_ref, wu_ref, wd_ref,
                 o_ref, act_ref):
    attn = jnp.concatenate([attn_ref[hp] for hp in range(attn_ref.shape[0])], axis=1)
    y_a = _dot(attn, wpa_ref[...])
    merged = ga_ref[...].astype(jnp.float32) * y_a + gyb_ref[...].astype(jnp.float32)
    x2 = x_ref[...] + _dot(_bf16(merged), wout_ref[...])
    o_ref[...] = _swiglu_half_step(x2, gain_ref[...], wg_ref, wu_ref, wd_ref, act_ref)


def _rows(tm, width):
    return pl.BlockSpec((tm, width), lambda i: (i, 0))


def _cols(height, tm):
    return pl.BlockSpec((height, tm), lambda i: (0, i))


def _resident():
    return pl.BlockSpec(memory_space=pltpu.VMEM)


def _params(n_axes):
    return pltpu.CompilerParams(dimension_semantics=("arbitrary",) * n_axes, vmem_limit_bytes=VMEM_LIMIT)


def _to_head_tiles(w, real):
    kdim = w.shape[0]
    w = w.reshape(kdim, -1, real)
    w = jnp.pad(w, ((0, 0), (0, 0), (0, 1)))
    src = np.where((HEAD_SRC >= 0) & (HEAD_SRC < real), HEAD_SRC, real)
    return w[:, :, src].reshape(kdim, -1)


def kernel(x, positions, ffn1_norm, ffn1_w_gate, ffn1_w_up, ffn1_w_down, mix_norm, w_in, gate_bias, q_a_norm, w_uq, kv_a_norm, w_uk, w_uv, q_head_norm, k_head_norm, w_proj_attn, conv_w, w_proj_conv, w_out, ffn2_norm, ffn2_w_gate, ffn2_w_up, ffn2_w_down):
    b, s, d = x.shape
    t = b * s
    assert d == D_MODEL and t % TM_FFN == 0 and s % TM_MIX == 0 and s % TQ == 0
    f32 = jnp.float32
    bf = jnp.bfloat16
    row = lambda g: g.reshape(1, -1).astype(f32)

    x_flat = x.reshape(t, d)
    pos_col = positions.reshape(t, 1).astype(f32)
    pos_row = positions.reshape(1, t).astype(f32)

    w_in_t = _bf16(w_in).T
    w_kr_t = jnp.zeros((HEAD_PAD, D_MODEL), bf)
    w_kr_t = w_kr_t.at[ROPE_LO:HALF_TILE].set(w_in_t[LATENT_DIM:LATENT_DIM + ROPE_HALF])
    w_kr_t = w_kr_t.at[HALF_TILE + ROPE_LO:].set(w_in_t[LATENT_DIM + ROPE_HALF:REST_START])
    w_lat_t = jnp.concatenate([w_in_t[:Q_LORA_RANK], w_kr_t, w_in_t[Q_LORA_RANK:LATENT_DIM]], axis=0)
    w_rest_t = w_in_t[REST_START:]
    compact = HEAD_SRC[HEAD_SRC >= 0]
    w_uq_t = _bf16(w_uq.reshape(Q_LORA_RANK, N_HEADS, QK_HEAD_DIM)[:, :, compact].reshape(Q_LORA_RANK, -1).T)
    w_uk_pad = _bf16(_to_head_tiles(w_uk, QK_NOPE_DIM))
    w_uv_t = _bf16(w_uv.T)
    head_gain = lambda g: _to_head_tiles(g.reshape(1, QK_HEAD_DIM).astype(f32), QK_HEAD_DIM)
    q_gain_cols = jnp.broadcast_to(q_head_norm.astype(f32)[compact].reshape(QK_HEAD_DIM, 1), (QK_HEAD_DIM, TM_MIX))
    inv_freq = 1.0 / (ROPE_THETA ** (jnp.arange(ROPE_HALF, dtype=f32) / ROPE_HALF))
    inv_freq_row = _to_head_tiles(
        jnp.concatenate([jnp.zeros((QK_NOPE_DIM,), f32), inv_freq, inv_freq]).reshape(1, QK_HEAD_DIM), QK_HEAD_DIM)
    inv_freq_cols = jnp.broadcast_to(inv_freq.reshape(ROPE_HALF, 1), (ROPE_HALF, TM_MIX))

    def ffn_weights(wg, wu, wd):
        return _bf16(wg), _bf16(wu), _bf16(wd)

    x1 = pl.pallas_call(
        _ffn_kernel,
        grid=(t // TM_FFN1,),
        in_specs=[_rows(TM_FFN1, d)] + [_resident()] * 4,
        out_specs=_rows(TM_FFN1, d),
        out_shape=jax.ShapeDtypeStruct((t, d), f32),
        scratch_shapes=[pltpu.VMEM((TM_FFN1, D_FF), bf)],
        compiler_params=_params(1),
        name="ffn1",
    )(x_flat, row(ffn1_norm), *ffn_weights(ffn1_w_gate, ffn1_w_up, ffn1_w_down))

    heads_w = N_HEADS * HEAD_PAD
    pair = 2 * HEAD_PAD
    n_pairs = N_HEADS // 2
    tile_major = pl.BlockSpec((None, heads_w, TM_MIX), lambda i: (i, 0, 0))
    q_t, k, v_t, gate_a, gated_yb = pl.pallas_call(
        functools.partial(_mix_kernel, s),
        grid=(t // TM_MIX,),
        in_specs=[_rows(TM_MIX, d), _rows(TM_MIX, 1), _cols(1, TM_MIX)] + [_resident()] * 15,
        out_specs=[tile_major, pl.BlockSpec((n_pairs, TM_MIX, pair), lambda i: (0, i, 0)), tile_major,
                   _rows(TM_MIX, d), _rows(TM_MIX, d)],
        out_shape=[jax.ShapeDtypeStruct((t // TM_MIX, heads_w, TM_MIX), bf), jax.ShapeDtypeStruct((n_pairs, t, pair), bf),
                   jax.ShapeDtypeStruct((t // TM_MIX, heads_w, TM_MIX), bf), jax.ShapeDtypeStruct((t, d), bf),
                   jax.ShapeDtypeStruct((t, d), bf)],
        scratch_shapes=[pltpu.VMEM((TM_MIX + 2 * SUBLANES, CONV_DIM), f32)],
        compiler_params=_params(1),
        name="mix",
    )(x1, pos_col, pos_row, inv_freq_row, inv_freq_cols, row(mix_norm), w_lat_t, w_rest_t, row(gate_bias), row(q_a_norm),
      w_uq_t, row(kv_a_norm), w_uk_pad, w_uv_t, q_gain_cols, head_gain(k_head_norm),
      conv_w.astype(f32), _bf16(w_proj_conv))

    pp = ATTN_PAIRS_PER_STEP
    feature_major = pl.BlockSpec((s // TM_MIX, pp * pair, TM_MIX), lambda bi, hp: (bi, hp, 0))
    attn = pl.pallas_call(
        _attn_kernel,
        grid=(b, n_pairs // pp),
        in_specs=[feature_major, pl.BlockSpec((pp, s, pair), lambda bi, hp: (hp, bi, 0)), feature_major],
        out_specs=pl.BlockSpec((pp, s, LANES), lambda bi, hp: (hp, bi, 0)),
        out_shape=jax.ShapeDtypeStruct((n_pairs, t, LANES), bf),
        compiler_params=_params(2),
        name="attn",
    )(q_t, k, v_t)

    out = pl.pallas_call(
        _post_kernel,
        grid=(t // TM_FFN,),
        in_specs=[_rows(TM_FFN, d), pl.BlockSpec((n_pairs, TM_FFN, LANES), lambda i: (0, i, 0)),
                  _rows(TM_FFN, d), _rows(TM_FFN, d)]
        + [_resident()] * 6,
        out_specs=_rows(TM_FFN, d),
        out_shape=jax.ShapeDtypeStruct((t, d), f32),
        scratch_shapes=[pltpu.VMEM((TM_FFN, D_FF), bf)],
        compiler_params=_params(1),
        name="post",
    )(x1, attn, gate_a, gated_yb, _bf16(w_proj_attn), _bf16(w_out), row(ffn2_norm),
      *ffn_weights(ffn2_w_gate, ffn2_w_up, ffn2_w_down))
    return out.reshape(b, s, d)
```

```python
import functools

import jax
import jax.numpy as jnp
import numpy as np
from jax import lax
from jax.experimental import pallas as pl
from jax.experimental.pallas import tpu as pltpu

D_MODEL = 1024
D_FF = 2816
N_HEADS = 8
QK_NOPE_DIM = 64
QK_ROPE_DIM = 32
QK_HEAD_DIM = QK_NOPE_DIM + QK_ROPE_DIM
V_HEAD_DIM = 64
Q_LORA_RANK = 384
KV_LORA_RANK = 256
CONV_DIM = 1024
CONV_WIDTH = 3
ROPE_THETA = 10000.0
NORM_EPS = 1e-6

LANES = 128
SUBLANES = 8
MXU_WIDTH = 256
HEAD_PAD = LANES
HALF_TILE = LANES // 2
ROPE_HALF = QK_ROPE_DIM // 2
ROPE_LO = HALF_TILE - ROPE_HALF


def _head_feature_sources():
    src = np.full((HEAD_PAD,), -1, np.int64)
    src[0:ROPE_LO] = np.arange(ROPE_LO)
    src[ROPE_LO:HALF_TILE] = QK_NOPE_DIM + np.arange(ROPE_HALF)
    rest = QK_NOPE_DIM - ROPE_LO
    src[HALF_TILE:HALF_TILE + rest] = ROPE_LO + np.arange(rest)
    src[HALF_TILE + ROPE_LO:HEAD_PAD] = QK_NOPE_DIM + ROPE_HALF + np.arange(ROPE_HALF)
    return src


HEAD_SRC = _head_feature_sources()

LATENT_DIM = Q_LORA_RANK + KV_LORA_RANK
REST_START = LATENT_DIM + QK_ROPE_DIM
OFF_XC = 0
OFF_GB = OFF_XC + CONV_DIM
OFF_GC = OFF_GB + CONV_DIM
OFF_GA_LOGIT = OFF_GC + CONV_DIM
OFF_GB_LOGIT = OFF_GA_LOGIT + D_MODEL

FF_CHUNKS = (768, 768, 768, 512)
VMEM_LIMIT = 56 * 1024 * 1024

TM_FFN = 512
TM_FFN1 = 1024
TM_MIX = 512
TQ = 256
ATTN_PAIRS_PER_STEP = 4
PROJ_AHEAD = 1
SCORE_LOOKAHEAD = 3
MASK_VALUE = -1e30
LOG2_E = 1.4426950408889634


def _bf16(x):
    return x.astype(jnp.bfloat16)


def _dot(a, b):
    return jnp.dot(a, b, preferred_element_type=jnp.float32)


def _dot_nt(a, b):
    return lax.dot_general(a, b, (((1,), (1,)), ((), ())), preferred_element_type=jnp.float32)


def _sum_sq(x):
    return jnp.sum(x * x, axis=-1, keepdims=True)


def _rmsnorm(x, gain, n):
    return x * lax.rsqrt(_sum_sq(x) * (1.0 / n) + NORM_EPS) * gain


def _sigmoid(x):
    return 0.5 * jnp.tanh(0.5 * x) + 0.5


def _swiglu_half_step(x, gain, wg_ref, wu_ref, wd_ref, act_ref):
    h = _bf16(_rmsnorm(x, gain, D_MODEL))
    c0 = 0
    for ck in FF_CHUNKS:
        g = _dot(h, wg_ref[:, c0:c0 + ck])
        u = _dot(h, wu_ref[:, c0:c0 + ck])
        act_ref[:, c0:c0 + ck] = _bf16(g * _sigmoid(g) * u)
        c0 += ck
    return x + 0.5 * _dot(act_ref[...], wd_ref[...])


def _ffn_kernel(x_ref, gain_ref, wg_ref, wu_ref, wd_ref, o_ref, act_ref):
    o_ref[...] = _swiglu_half_step(x_ref[...], gain_ref[...], wg_ref, wu_ref, wd_ref, act_ref)


def _swap_halves(t):
    return pltpu.roll(t, HALF_TILE, 1)


def _rope_tables_token_major(pos, inv_freq):
    half_rows = pos.shape[0] // 2
    lane = lax.broadcasted_iota(jnp.int32, (half_rows, LANES), 1)
    low = lane < HALF_TILE
    ang = jnp.where(low, pos[:half_rows], pos[half_rows:]) * inv_freq
    cos_p, sin_p = jnp.cos(ang), jnp.sin(ang)
    cos_s, sin_s = _swap_halves(cos_p), _swap_halves(sin_p)
    cos_t = jnp.concatenate([jnp.where(low, cos_p, cos_s), jnp.where(low, cos_s, cos_p)], axis=0)
    sin_t = jnp.concatenate([jnp.where(low, -sin_p, sin_s), jnp.where(low, -sin_s, sin_p)], axis=0)
    return cos_t, sin_t


def _q_head_feature_major(x, gain, cos, sin):
    inv = lax.rsqrt(jnp.sum(x * x, axis=0, keepdims=True) * (1.0 / QK_HEAD_DIM) + NORM_EPS)
    y = x * gain
    lo2 = HALF_TILE + ROPE_HALF
    y1, y2 = y[ROPE_LO:HALF_TILE], y[lo2:QK_HEAD_DIM]
    pad = jnp.zeros((HEAD_PAD - QK_HEAD_DIM, x.shape[1]), jnp.float32)
    return jnp.concatenate([y[0:ROPE_LO] * inv, (y1 * cos - y2 * sin) * inv, y[HALF_TILE:lo2] * inv, pad,
                            (y2 * cos + y1 * sin) * inv], axis=0)


def _mix_kernel(seq_len, x_ref, pos_col_ref, pos_row_ref, invf_row_ref, invf_col_ref, mixg_ref, wlat_t_ref,
                wrest_t_ref, bias_ref, qag_ref, wuqt_ref, kvag_ref, wuk_ref, wuvt_ref, qhg_ref, khg_ref,
                convw_ref, wpc_ref, qt_ref, k_ref, vt_ref, ga_ref, gyb_ref, ubuf_ref):
    tm = x_ref.shape[0]
    h = _bf16(_rmsnorm(x_ref[...], mixg_ref[...], D_MODEL))

    def proj(off, width):
        return _dot_nt(h, wrest_t_ref[off:off + width, :])

    @pl.when((pl.program_id(0) * tm) % seq_len == 0)
    def _():
        ubuf_ref[0:SUBLANES, :] = jnp.zeros((SUBLANES, CONV_DIM), jnp.float32)

    n_chunks = CONV_DIM // MXU_WIDTH
    heads_per_chunk = N_HEADS // n_chunks
    y_b = []
    gate_b = []

    def conv_projections(c):
        return tuple(proj(off + c * MXU_WIDTH, MXU_WIDTH) for off in (OFF_GC, OFF_XC, OFF_GB))

    def conv_chunk(c, projections):
        p_gc, p_xc, p_gb = projections
        cs = slice(c * MXU_WIDTH, (c + 1) * MXU_WIDTH)
        u = p_gc * p_xc
        ubuf_ref[SUBLANES:SUBLANES + tm, cs] = u
        z = (convw_ref[0:1, cs] * ubuf_ref[SUBLANES - 2:SUBLANES - 2 + tm, cs]
             + convw_ref[1:2, cs] * ubuf_ref[SUBLANES - 1:SUBLANES - 1 + tm, cs]
             + convw_ref[2:3, cs] * u)
        ubuf_ref[0:SUBLANES, cs] = ubuf_ref[tm:tm + SUBLANES, cs]
        y_b.append(_bf16(p_gb * z))

    def gate_projections(c):
        return tuple(proj(off + c * MXU_WIDTH, MXU_WIDTH) for off in (OFF_GA_LOGIT, OFF_GB_LOGIT))

    def gate_chunk(c, projections):
        p_ga, p_gbl = projections
        cs = slice(c * MXU_WIDTH, (c + 1) * MXU_WIDTH)
        ga_ref[:, cs] = _bf16(_sigmoid(p_ga + bias_ref[:, cs]))
        gate_b.append(_sigmoid(p_gbl + bias_ref[:, D_MODEL + cs.start:D_MODEL + cs.stop]))

    def q_heads(tokens, q_t):
        for hd in range(N_HEADS):
            qt_ref[hd * HEAD_PAD:(hd + 1) * HEAD_PAD, tokens] = _bf16(_q_head_feature_major(
                q_t[hd * QK_HEAD_DIM:(hd + 1) * QK_HEAD_DIM, :], q_gain[:, tokens], cos_q[:, tokens], sin_q[:, tokens]))

    def k_heads(c, k_nope):
        for i in range(heads_per_chunk):
            kn = k_nope[:, i * HEAD_PAD:(i + 1) * HEAD_PAD]
            inv = lax.rsqrt((_sum_sq(kn) + k_rope_sq) * (1.0 / QK_HEAD_DIM) + NORM_EPS)
            k_ref[c, :, i * HEAD_PAD:(i + 1) * HEAD_PAD] = _bf16((kn * k_gain + k_rope_rot) * inv)

    q_kr = _dot_nt(h, wlat_t_ref[0:Q_LORA_RANK + HEAD_PAD, :])
    q_lat = q_kr[:, 0:Q_LORA_RANK]
    k_rope = q_kr[:, Q_LORA_RANK:]
    kv_lat = _dot_nt(h, wlat_t_ref[Q_LORA_RANK + HEAD_PAD:, :])
    conv_p = [conv_projections(c) for c in range(PROJ_AHEAD)]

    ang_q = invf_col_ref[...] * pos_row_ref[...]
    cos_q, sin_q = jnp.cos(ang_q), jnp.sin(ang_q)
    q_gain = qhg_ref[...] * (QK_HEAD_DIM ** -0.5 * LOG2_E)
    cos_k, sin_k = _rope_tables_token_major(pos_col_ref[...], invf_row_ref[...])
    k_gain = khg_ref[...]
    k_rope_g = k_rope * k_gain
    k_rope_rot = k_rope_g * cos_k + _swap_halves(k_rope_g) * sin_k
    k_rope_sq = _sum_sq(k_rope)

    qn = _bf16(_rmsnorm(q_lat, qag_ref[...], Q_LORA_RANK))
    ckv = _bf16(_rmsnorm(kv_lat, kvag_ref[...], KV_LORA_RANK))
    gate_p = []
    half = tm // 2
    for c in range(n_chunks):
        if c % 2 == 0:
            tokens = slice((c // 2) * half, (c // 2 + 1) * half)
            q_heads(tokens, _dot_nt(wuqt_ref[...], qn[tokens]))
        k_heads(c, _dot(ckv, wuk_ref[:, c * MXU_WIDTH:(c + 1) * MXU_WIDTH]))
        if c + PROJ_AHEAD < n_chunks:
            conv_p.append(conv_projections(c + PROJ_AHEAD))
        gate_p.append(gate_projections(c))
    v_t = _bf16(_dot_nt(wuvt_ref[...], ckv))
    pad_rows = lax.broadcasted_iota(jnp.int32, (HEAD_PAD - V_HEAD_DIM, tm), 0)
    ones_block = jnp.where(pad_rows == 0, 1.0, 0.0).astype(jnp.bfloat16)
    for hd in range(N_HEADS):
        vt_ref[hd * HEAD_PAD:hd * HEAD_PAD + V_HEAD_DIM, :] = v_t[hd * V_HEAD_DIM:(hd + 1) * V_HEAD_DIM, :]
        vt_ref[hd * HEAD_PAD + V_HEAD_DIM:(hd + 1) * HEAD_PAD, :] = ones_block

    for c in range(n_chunks):
        conv_chunk(c, conv_p[c])
        gate_chunk(c, gate_p[c])
    y_conv = _dot(jnp.concatenate(y_b, axis=1), wpc_ref[...])
    for c, g in enumerate(gate_b):
        cs = slice(c * MXU_WIDTH, (c + 1) * MXU_WIDTH)
        gyb_ref[:, cs] = _bf16(g * y_conv[:, cs])


def _attn_kernel(qt_ref, k_ref, vt_ref, o_ref):
    n_pairs, s_len, _ = k_ref.shape
    key = lax.broadcasted_iota(jnp.int32, (TQ, TQ), 0)
    qry = lax.broadcasted_iota(jnp.int32, (TQ, TQ), 1)
    n_q = s_len // TQ
    tm = qt_ref.shape[2]

    def tokens(ref, hd, lo, hi):
        hs = slice(hd * HEAD_PAD, (hd + 1) * HEAD_PAD)
        parts = [ref[j, hs, max(lo - j * tm, 0):min(hi - j * tm, tm)] for j in range(lo // tm, -(-hi // tm))]
        return parts[0] if len(parts) == 1 else jnp.concatenate(parts, axis=1)

    def scores(qi, hd):
        lo, hi = qi * TQ, (qi + 1) * TQ
        k = k_ref[hd // 2, 0:hi, (hd % 2) * HEAD_PAD:(hd % 2 + 1) * HEAD_PAD]
        s = _dot(k, tokens(qt_ref, hd, lo, hi))
        s_diag = jnp.where(key <= qry, s[lo:hi, :], MASK_VALUE)
        s = s_diag if qi == 0 else jnp.concatenate([s[0:lo, :], s_diag], axis=0)
        return s, jnp.max(s, axis=0, keepdims=True)

    def finish(qi, hd, s_and_max):
        hi = (qi + 1) * TQ
        s, m = s_and_max
        p = jnp.exp2(s - m)
        acc = _dot(tokens(vt_ref, hd, 0, hi), _bf16(p))
        return acc[0:V_HEAD_DIM, :] / acc[V_HEAD_DIM:V_HEAD_DIM + 1, :]

    chains = [(qi, hd) for qi in reversed(range(n_q)) for hd in range(2 * n_pairs)]
    pending = [scores(*c) for c in chains[:SCORE_LOOKAHEAD]]
    outs = []
    for i, (qi, hd) in enumerate(chains):
        s = pending.pop(0)
        if i + SCORE_LOOKAHEAD < len(chains):
            pending.append(scores(*chains[i + SCORE_LOOKAHEAD]))
        outs.append(finish(qi, hd, s))
        if hd % 2 == 1:
            o_ref[hd // 2, qi * TQ:(qi + 1) * TQ, :] = _bf16(jnp.concatenate(outs, axis=0).T)
            outs = []


def _post_kernel(x_ref, attn_ref, ga_ref, gyb_ref, wpa_ref, wout_ref, gain_ref, wg_ref, wu_ref, wd_ref,
                 o_ref, act_ref):
    attn = jnp.concatenate([attn_ref[hp] for hp in range(attn_ref.shape[0])], axis=1)
    y_a = _dot(attn, wpa_ref[...])
    merged = ga_ref[...].astype(jnp.float32) * y_a + gyb_ref[...].astype(jnp.float32)
    x2 = x_ref[...] + _dot(_bf16(merged), wout_ref[...])
    o_ref[...] = _swiglu_half_step(x2, gain_ref[...], wg_ref, wu_ref, wd_ref, act_ref)


def _rows(tm, width):
    return pl.BlockSpec((tm, width), lambda i: (i, 0))


def _cols(height, tm):
    return pl.BlockSpec((height, tm), lambda i: (0, i))


def _resident():
    return pl.BlockSpec(memory_space=pltpu.VMEM)


def _params(n_axes):
    return pltpu.CompilerParams(dimension_semantics=("arbitrary",) * n_axes, vmem_limit_bytes=VMEM_LIMIT)


def _to_head_tiles(w, real):
    kdim = w.shape[0]
    w = w.reshape(kdim, -1, real)
    w = jnp.pad(w, ((0, 0), (0, 0), (0, 1)))
    src = np.where((HEAD_SRC >= 0) & (HEAD_SRC < real), HEAD_SRC, real)
    return w[:, :, src].reshape(kdim, -1)


def kernel(x, positions, ffn1_norm, ffn1_w_gate, ffn1_w_up, ffn1_w_down, mix_norm, w_in, gate_bias, q_a_norm, w_uq, kv_a_norm, w_uk, w_uv, q_head_norm, k_head_norm, w_proj_attn, conv_w, w_proj_conv, w_out, ffn2_norm, ffn2_w_gate, ffn2_w_up, ffn2_w_down):
    b, s, d = x.shape
    t = b * s
    assert d == D_MODEL and t % TM_FFN == 0 and s % TM_MIX == 0 and s % TQ == 0
    f32 = jnp.float32
    bf = jnp.bfloat16
    row = lambda g: g.reshape(1, -1).astype(f32)

    x_flat = x.reshape(t, d)
    pos_col = positions.reshape(t, 1).astype(f32)
    pos_row = positions.reshape(1, t).astype(f32)

    w_in_t = _bf16(w_in).T
    w_kr_t = jnp.zeros((HEAD_PAD, D_MODEL), bf)
    w_kr_t = w_kr_t.at[ROPE_LO:HALF_TILE].set(w_in_t[LATENT_DIM:LATENT_DIM + ROPE_HALF])
    w_kr_t = w_kr_t.at[HALF_TILE + ROPE_LO:].set(w_in_t[LATENT_DIM + ROPE_HALF:REST_START])
    w_lat_t = jnp.concatenate([w_in_t[:Q_LORA_RANK], w_kr_t, w_in_t[Q_LORA_RANK:LATENT_DIM]], axis=0)
    w_rest_t = w_in_t[REST_START:]
    compact = HEAD_SRC[HEAD_SRC >= 0]
    w_uq_t = _bf16(w_uq.reshape(Q_LORA_RANK, N_HEADS, QK_HEAD_DIM)[:, :, compact].reshape(Q_LORA_RANK, -1).T)
    w_uk_pad = _bf16(_to_head_tiles(w_uk, QK_NOPE_DIM))
    w_uv_t = _bf16(w_uv.T)
    head_gain = lambda g: _to_head_tiles(g.reshape(1, QK_HEAD_DIM).astype(f32), QK_HEAD_DIM)
    q_gain_cols = jnp.broadcast_to(q_head_norm.astype(f32)[compact].reshape(QK_HEAD_DIM, 1), (QK_HEAD_DIM, TM_MIX))
    inv_freq = 1.0 / (ROPE_THETA ** (jnp.arange(ROPE_HALF, dtype=f32) / ROPE_HALF))
    inv_freq_row = _to_head_tiles(
        jnp.concatenate([jnp.zeros((QK_NOPE_DIM,), f32), inv_freq, inv_freq]).reshape(1, QK_HEAD_DIM), QK_HEAD_DIM)
    inv_freq_cols = jnp.broadcast_to(inv_freq.reshape(ROPE_HALF, 1), (ROPE_HALF, TM_MIX))

    def ffn_weights(wg, wu, wd):
        return _bf16(wg), _bf16(wu), _bf16(wd)

    x1 = pl.pallas_call(
        _ffn_kernel,
        grid=(t // TM_FFN1,),
        in_specs=[_rows(TM_FFN1, d)] + [_resident()] * 4,
        out_specs=_rows(TM_FFN1, d),
        out_shape=jax.ShapeDtypeStruct((t, d), f32),
        scratch_shapes=[pltpu.VMEM((TM_FFN1, D_FF), bf)],
        compiler_params=_params(1),
        name="ffn1",
    )(x_flat, row(ffn1_norm), *ffn_weights(ffn1_w_gate, ffn1_w_up, ffn1_w_down))

    heads_w = N_HEADS * HEAD_PAD
    pair = 2 * HEAD_PAD
    n_pairs = N_HEADS // 2
    tile_major = pl.BlockSpec((None, heads_w, TM_MIX), lambda i: (i, 0, 0))
    q_t, k, v_t, gate_a, gated_yb = pl.pallas_call(
        functools.partial(_mix_kernel, s),
        grid=(t // TM_MIX,),
        in_specs=[_rows(TM_MIX, d), _rows(TM_MIX, 1), _cols(1, TM_MIX)] + [_resident()] * 15,
        out_specs=[tile_major, pl.BlockSpec((n_pairs, TM_MIX, pair), lambda i: (0, i, 0)), tile_major,
                   _rows(TM_MIX, d), _rows(TM_MIX, d)],
        out_shape=[jax.ShapeDtypeStruct((t // TM_MIX, heads_w, TM_MIX), bf), jax.ShapeDtypeStruct((n_pairs, t, pair), bf),
                   jax.ShapeDtypeStruct((t // TM_MIX, heads_w, TM_MIX), bf), jax.ShapeDtypeStruct((t, d), bf),
                   jax.ShapeDtypeStruct((t, d), bf)],
        scratch_shapes=[pltpu.VMEM((TM_MIX + 2 * SUBLANES, CONV_DIM), f32)],
        compiler_params=_params(1),
        name="mix",
    )(x1, pos_col, pos_row, inv_freq_row, inv_freq_cols, row(mix_norm), w_lat_t, w_rest_t, row(gate_bias), row(q_a_norm),
      w_uq_t, row(kv_a_norm), w_uk_pad, w_uv_t, q_gain_cols, head_gain(k_head_norm),
      conv_w.astype(f32), _bf16(w_proj_conv))

    pp = ATTN_PAIRS_PER_STEP
    feature_major = pl.BlockSpec((s // TM_MIX, pp * pair, TM_MIX), lambda bi, hp: (bi, hp, 0))
    attn = pl.pallas_call(
        _attn_kernel,
        grid=(b, n_pairs // pp),
        in_specs=[feature_major, pl.BlockSpec((pp, s, pair), lambda bi, hp: (hp, bi, 0)), feature_major],
        out_specs=pl.BlockSpec((pp, s, LANES), lambda bi, hp: (hp, bi, 0)),
        out_shape=jax.ShapeDtypeStruct((n_pairs, t, LANES), bf),
        compiler_params=_params(2),
        name="attn",
    )(q_t, k, v_t)

    out = pl.pallas_call(
        _post_kernel,
        grid=(t // TM_FFN,),
        in_specs=[_rows(TM_FFN, d), pl.BlockSpec((n_pairs, TM_FFN, LANES), lambda i: (0, i, 0)),
                  _rows(TM_FFN, d), _rows(TM_FFN, d)]
        + [_resident()] * 6,
        out_specs=_rows(TM_FFN, d),
        out_shape=jax.ShapeDtypeStruct((t, d), f32),
        scratch_shapes=[pltpu.VMEM((TM_FFN, D_FF), bf)],
        compiler_params=_params(1),
        name="post",
    )(x1, attn, gate_a, gated_yb, _bf16(w_proj_attn), _bf16(w_out), row(ffn2_norm),
      *ffn_weights(ffn2_w_gate, ffn2_w_up, ffn2_w_down))
    return out.reshape(b, s, d)
```

```python
import functools

import jax
import jax.numpy as jnp
import numpy as np
from jax import lax
from jax.experimental import pallas as pl
from jax.experimental.pallas import tpu as pltpu

D_MODEL = 1024
D_FF = 2816
N_HEADS = 8
QK_NOPE_DIM = 64
QK_ROPE_DIM = 32
QK_HEAD_DIM = QK_NOPE_DIM + QK_ROPE_DIM
V_HEAD_DIM = 64
Q_LORA_RANK = 384
KV_LORA_RANK = 256
CONV_DIM = 1024
CONV_WIDTH = 3
ROPE_THETA = 10000.0
NORM_EPS = 1e-6

LANES = 128
SUBLANES = 8
MXU_WIDTH = 256
CHUNK = MXU_WIDTH
HEAD_PAD = LANES
HALF_TILE = LANES // 2
ROPE_HALF = QK_ROPE_DIM // 2
ROPE_LO = HALF_TILE - ROPE_HALF


def _head_feature_sources():
    src = np.full((HEAD_PAD,), -1, np.int64)
    src[0:ROPE_LO] = np.arange(ROPE_LO)
    src[ROPE_LO:HALF_TILE] = QK_NOPE_DIM + np.arange(ROPE_HALF)
    rest = QK_NOPE_DIM - ROPE_LO
    src[HALF_TILE:HALF_TILE + rest] = ROPE_LO + np.arange(rest)
    src[HALF_TILE + ROPE_LO:HEAD_PAD] = QK_NOPE_DIM + ROPE_HALF + np.arange(ROPE_HALF)
    return src


HEAD_SRC = _head_feature_sources()

LATENT_DIM = Q_LORA_RANK + KV_LORA_RANK
REST_START = LATENT_DIM + QK_ROPE_DIM
OFF_XC = 0
OFF_GB = OFF_XC + CONV_DIM
OFF_GC = OFF_GB + CONV_DIM
OFF_GA_LOGIT = OFF_GC + CONV_DIM
OFF_GB_LOGIT = OFF_GA_LOGIT + D_MODEL

FF_CHUNKS = (768, 768, 768, 512)
VMEM_LIMIT = 56 * 1024 * 1024

TM_FFN = 512
WEIGHT_CHUNKS = 16
TM_FFN1 = 1024
TM_MIX = 512
TQ = 256
ATTN_PAIRS_PER_STEP = 4
PROJ_AHEAD = 1
SCORE_LOOKAHEAD = 3
MASK_VALUE = -1e30
LOG2_E = 1.4426950408889634


def _bf16(x):
    return x.astype(jnp.bfloat16)


def _dot(a, b):
    return jnp.dot(a, b, preferred_element_type=jnp.float32)


def _dot_nt(a, b):
    return lax.dot_general(a, b, (((1,), (1,)), ((), ())), preferred_element_type=jnp.float32)


def _sum_sq(x):
    return jnp.sum(x * x, axis=-1, keepdims=True)


def _rmsnorm(x, gain, n):
    return x * lax.rsqrt(_sum_sq(x) * (1.0 / n) + NORM_EPS) * gain


def _sigmoid(x):
    return 0.5 * jnp.tanh(0.5 * x) + 0.5


def _swiglu_half_step(x, gain, wg_ref, wu_ref, wd_ref, act_ref):
    h = _bf16(_rmsnorm(x, gain, D_MODEL))
    c0 = 0
    for ck in FF_CHUNKS:
        g = _dot(h, wg_ref[:, c0:c0 + ck])
        u = _dot(h, wu_ref[:, c0:c0 + ck])
        act_ref[:, c0:c0 + ck] = _bf16(g * _sigmoid(g) * u)
        c0 += ck
    return x + 0.5 * _dot(act_ref[...], wd_ref[...])


def _stage_copy(w_hbm, stage_ref, sem_ref, chunk, slot):
    rows = stage_ref.shape[1]
    return pltpu.make_async_copy(w_hbm.at[pl.ds(chunk * rows, rows)], stage_ref.at[slot], sem_ref.at[slot])


def _load_weight_bf16(w_hbm, w_bf_ref, stage_ref, sem_ref):
    rows = stage_ref.shape[1]
    n_chunks = w_hbm.shape[0] // rows
    _stage_copy(w_hbm, stage_ref, sem_ref, 0, 0).start()
    for c in range(n_chunks):
        slot = c % 2
        if c + 1 < n_chunks:
            _stage_copy(w_hbm, stage_ref, sem_ref, c + 1, 1 - slot).start()
        _stage_copy(w_hbm, stage_ref, sem_ref, c, slot).wait()
        w_bf_ref[c * rows:(c + 1) * rows, :] = _bf16(stage_ref[slot])


def _load_ffn_weights(wg_hbm, wu_hbm, wd_hbm, wg_ref, wu_ref, wd_ref, stage_in_ref, stage_out_ref, sem_ref):
    @pl.when(pl.program_id(0) == 0)
    def _():
        _load_weight_bf16(wg_hbm, wg_ref, stage_in_ref, sem_ref)
        _load_weight_bf16(wu_hbm, wu_ref, stage_in_ref, sem_ref)
        _load_weight_bf16(wd_hbm, wd_ref, stage_out_ref, sem_ref)


def _ffn_kernel(x_ref, gain_ref, wg_hbm, wu_hbm, wd_hbm, o_ref, act_ref, *weight_scratch):
    _load_ffn_weights(wg_hbm, wu_hbm, wd_hbm, *weight_scratch)
    wg_ref, wu_ref, wd_ref = weight_scratch[:3]
    o_ref[...] = _swiglu_half_step(x_ref[...], gain_ref[...], wg_ref, wu_ref, wd_ref, act_ref)


def _swap_halves(t):
    return pltpu.roll(t, HALF_TILE, 1)


def _rope_tables_token_major(pos, inv_freq):
    half_rows = pos.shape[0] // 2
    lane = lax.broadcasted_iota(jnp.int32, (half_rows, LANES), 1)
    low = lane < HALF_TILE
    ang = jnp.where(low, pos[:half_rows], pos[half_rows:]) * inv_freq
    cos_p, sin_p = jnp.cos(ang), jnp.sin(ang)
    cos_s, sin_s = _swap_halves(cos_p), _swap_halves(sin_p)
    cos_t = jnp.concatenate([jnp.where(low, cos_p, cos_s), jnp.where(low, cos_s, cos_p)], axis=0)
    sin_t = jnp.concatenate([jnp.where(low, -sin_p, sin_s), jnp.where(low, -sin_s, sin_p)], axis=0)
    return cos_t, sin_t


def _q_head_feature_major(x, gain, cos, sin):
    inv = lax.rsqrt(jnp.sum(x * x, axis=0, keepdims=True) * (1.0 / QK_HEAD_DIM) + NORM_EPS)
    y = x * gain
    lo2 = HALF_TILE + ROPE_HALF
    y1, y2 = y[ROPE_LO:HALF_TILE], y[lo2:QK_HEAD_DIM]
    pad = jnp.zeros((HEAD_PAD - QK_HEAD_DIM, x.shape[1]), jnp.float32)
    return jnp.concatenate([y[0:ROPE_LO] * inv, (y1 * cos - y2 * sin) * inv, y[HALF_TILE:lo2] * inv, pad,
                            (y2 * cos + y1 * sin) * inv], axis=0)


def _mix_kernel(seq_len, x_ref, pos_col_ref, pos_row_ref, invf_row_ref, invf_col_ref, mixg_ref, wlat_t_ref,
                wrest_t_ref, bias_ref, qag_ref, wuqt_ref, kvag_ref, wuk_ref, wuvt_ref, qhg_ref, khg_ref,
                convw_ref, wpc_ref, qt_ref, k_ref, vt_ref, ga_ref, gyb_ref, ubuf_ref):
    tm = x_ref.shape[0]
    h = _bf16(_rmsnorm(x_ref[...], mixg_ref[...], D_MODEL))

    def proj(off, width):
        return _dot_nt(h, wrest_t_ref[off:off + width, :])

    @pl.when((pl.program_id(0) * tm) % seq_len == 0)
    def _():
        ubuf_ref[0:SUBLANES, :] = jnp.zeros((SUBLANES, CONV_DIM), jnp.float32)

    n_chunks = CONV_DIM // CHUNK
    heads_per_chunk = N_HEADS // n_chunks
    y_b = []
    gate_b = []

    def conv_projections(c):
        return tuple(proj(off + c * CHUNK, CHUNK) for off in (OFF_GC, OFF_XC, OFF_GB))

    def conv_chunk(c, projections):
        p_gc, p_xc, p_gb = projections
        cs = slice(c * CHUNK, (c + 1) * CHUNK)
        u = p_gc * p_xc
        ubuf_ref[SUBLANES:SUBLANES + tm, cs] = u
        z = (convw_ref[0:1, cs] * ubuf_ref[SUBLANES - 2:SUBLANES - 2 + tm, cs]
             + convw_ref[1:2, cs] * ubuf_ref[SUBLANES - 1:SUBLANES - 1 + tm, cs]
             + convw_ref[2:3, cs] * u)
        ubuf_ref[0:SUBLANES, cs] = ubuf_ref[tm:tm + SUBLANES, cs]
        y_b.append(_bf16(p_gb * z))

    def gate_projections(c):
        return tuple(proj(off + c * CHUNK, CHUNK) for off in (OFF_GA_LOGIT, OFF_GB_LOGIT))

    def gate_chunk(c, projections):
        p_ga, p_gbl = projections
        cs = slice(c * CHUNK, (c + 1) * CHUNK)
        ga_ref[:, cs] = _bf16(_sigmoid(p_ga + bias_ref[:, cs]))
        gate_b.append(_sigmoid(p_gbl + bias_ref[:, D_MODEL + cs.start:D_MODEL + cs.stop]))

    def q_heads(tokens, q_t):
        for hd in range(N_HEADS):
            qt_ref[hd * HEAD_PAD:(hd + 1) * HEAD_PAD, tokens] = _bf16(_q_head_feature_major(
                q_t[hd * QK_HEAD_DIM:(hd + 1) * QK_HEAD_DIM, :], q_gain[:, tokens], cos_q[:, tokens], sin_q[:, tokens]))

    def k_heads(c, k_nope):
        for i in range(heads_per_chunk):
            kn = k_nope[:, i * HEAD_PAD:(i + 1) * HEAD_PAD]
            inv = lax.rsqrt((_sum_sq(kn) + k_rope_sq) * (1.0 / QK_HEAD_DIM) + NORM_EPS)
            hd = c * heads_per_chunk + i
            k_ref[hd // 2, :, (hd % 2) * HEAD_PAD:(hd % 2 + 1) * HEAD_PAD] = _bf16((kn * k_gain + k_rope_rot) * inv)

    q_kr = _dot_nt(h, wlat_t_ref[0:Q_LORA_RANK + HEAD_PAD, :])
    q_lat = q_kr[:, 0:Q_LORA_RANK]
    k_rope = q_kr[:, Q_LORA_RANK:]
    kv_lat = _dot_nt(h, wlat_t_ref[Q_LORA_RANK + HEAD_PAD:, :])
    conv_p = [conv_projections(c) for c in range(PROJ_AHEAD)]

    ang_q = invf_col_ref[...] * pos_row_ref[...]
    cos_q, sin_q = jnp.cos(ang_q), jnp.sin(ang_q)
    q_gain = qhg_ref[...] * (QK_HEAD_DIM ** -0.5 * LOG2_E)
    cos_k, sin_k = _rope_tables_token_major(pos_col_ref[...], invf_row_ref[...])
    k_gain = khg_ref[...]
    k_rope_g = k_rope * k_gain
    k_rope_rot = k_rope_g * cos_k + _swap_halves(k_rope_g) * sin_k
    k_rope_sq = _sum_sq(k_rope)

    qn = _bf16(_rmsnorm(q_lat, qag_ref[...], Q_LORA_RANK))
    ckv = _bf16(_rmsnorm(kv_lat, kvag_ref[...], KV_LORA_RANK))
    gate_p = []
    half = tm // 2
    for c in range(n_chunks):
        if c % (n_chunks // 2) == 0:
            wave = c // (n_chunks // 2)
            tokens = slice(wave * half, (wave + 1) * half)
            q_heads(tokens, _dot_nt(wuqt_ref[...], qn[tokens]))
        k_heads(c, _dot(ckv, wuk_ref[:, c * CHUNK:(c + 1) * CHUNK]))
        if c + PROJ_AHEAD < n_chunks:
            conv_p.append(conv_projections(c + PROJ_AHEAD))
        gate_p.append(gate_projections(c))
    v_t = _bf16(_dot_nt(wuvt_ref[...], ckv))
    pad_rows = lax.broadcasted_iota(jnp.int32, (HEAD_PAD - V_HEAD_DIM, tm), 0)
    ones_block = jnp.where(pad_rows == 0, 1.0, 0.0).astype(jnp.bfloat16)
    for hd in range(N_HEADS):
        vt_ref[hd * HEAD_PAD:hd * HEAD_PAD + V_HEAD_DIM, :] = v_t[hd * V_HEAD_DIM:(hd + 1) * V_HEAD_DIM, :]
        vt_ref[hd * HEAD_PAD + V_HEAD_DIM:(hd + 1) * HEAD_PAD, :] = ones_block

    for c in range(n_chunks):
        conv_chunk(c, conv_p[c])
        gate_chunk(c, gate_p[c])
    y_conv = _dot(jnp.concatenate(y_b, axis=1), wpc_ref[...])
    for c, g in enumerate(gate_b):
        cs = slice(c * CHUNK, (c + 1) * CHUNK)
        gyb_ref[:, cs] = _bf16(g * y_conv[:, cs])


def _attn_kernel(qt_ref, k_ref, vt_ref, o_ref):
    n_pairs, s_len, _ = k_ref.shape
    key = lax.broadcasted_iota(jnp.int32, (TQ, TQ), 0)
    qry = lax.broadcasted_iota(jnp.int32, (TQ, TQ), 1)
    n_q = s_len // TQ
    tm = qt_ref.shape[2]

    def tokens(ref, hd, lo, hi):
        hs = slice(hd * HEAD_PAD, (hd + 1) * HEAD_PAD)
        parts = [ref[j, hs, max(lo - j * tm, 0):min(hi - j * tm, tm)] for j in range(lo // tm, -(-hi // tm))]
        return parts[0] if len(parts) == 1 else jnp.concatenate(parts, axis=1)

    def scores(qi, hd):
        lo, hi = qi * TQ, (qi + 1) * TQ
        k = k_ref[hd // 2, 0:hi, (hd % 2) * HEAD_PAD:(hd % 2 + 1) * HEAD_PAD]
        s = _dot(k, tokens(qt_ref, hd, lo, hi))
        s_diag = jnp.where(key <= qry, s[lo:hi, :], MASK_VALUE)
        s = s_diag if qi == 0 else jnp.concatenate([s[0:lo, :], s_diag], axis=0)
        return s, jnp.max(s, axis=0, keepdims=True)

    def finish(qi, hd, s_and_max):
        hi = (qi + 1) * TQ
        s, m = s_and_max
        p = jnp.exp2(s - m)
        acc = _dot(tokens(vt_ref, hd, 0, hi), _bf16(p))
        return acc[0:V_HEAD_DIM, :] / acc[V_HEAD_DIM:V_HEAD_DIM + 1, :]

    chains = [(qi, hd) for qi in reversed(range(n_q)) for hd in range(2 * n_pairs)]
    pending = [scores(*c) for c in chains[:SCORE_LOOKAHEAD]]
    outs = []
    for i, (qi, hd) in enumerate(chains):
        s = pending.pop(0)
        if i + SCORE_LOOKAHEAD < len(chains):
            pending.append(scores(*chains[i + SCORE_LOOKAHEAD]))
        outs.append(finish(qi, hd, s))
        if hd % 2 == 1:
            o_ref[hd // 2, qi * TQ:(qi + 1) * TQ, :] = _bf16(jnp.concatenate(outs, axis=0).T)
            outs = []


def _post_kernel(x_ref, attn_ref, ga_ref, gyb_ref, wpa_ref, wout_ref, gain_ref, wg_hbm, wu_hbm, wd_hbm,
                 o_ref, act_ref, *weight_scratch):
    _load_ffn_weights(wg_hbm, wu_hbm, wd_hbm, *weight_scratch)
    wg_ref, wu_ref, wd_ref = weight_scratch[:3]
    attn = jnp.concatenate([attn_ref[hp] for hp in range(attn_ref.shape[0])], axis=1)
    y_a = _dot(attn, wpa_ref[...])
    merged = ga_ref[...].astype(jnp.float32) * y_a + gyb_ref[...].astype(jnp.float32)
    x2 = x_ref[...] + _dot(_bf16(merged), wout_ref[...])
    o_ref[...] = _swiglu_half_step(x2, gain_ref[...], wg_ref, wu_ref, wd_ref, act_ref)


def _rows(tm, width):
    return pl.BlockSpec((tm, width), lambda i: (i, 0))


def _cols(height, tm):
    return pl.BlockSpec((height, tm), lambda i: (0, i))


def _resident():
    return pl.BlockSpec(memory_space=pltpu.VMEM)


def _params(n_axes):
    return pltpu.CompilerParams(dimension_semantics=("arbitrary",) * n_axes, vmem_limit_bytes=VMEM_LIMIT)


def _to_head_tiles(w, real):
    kdim = w.shape[0]
    w = w.reshape(kdim, -1, real)
    w = jnp.pad(w, ((0, 0), (0, 0), (0, 1)))
    src = np.where((HEAD_SRC >= 0) & (HEAD_SRC < real), HEAD_SRC, real)
    return w[:, :, src].reshape(kdim, -1)


def kernel(x, positions, ffn1_norm, ffn1_w_gate, ffn1_w_up, ffn1_w_down, mix_norm, w_in, gate_bias, q_a_norm, w_uq, kv_a_norm, w_uk, w_uv, q_head_norm, k_head_norm, w_proj_attn, conv_w, w_proj_conv, w_out, ffn2_norm, ffn2_w_gate, ffn2_w_up, ffn2_w_down):
    b, s, d = x.shape
    t = b * s
    assert d == D_MODEL and t % TM_FFN == 0 and s % TM_MIX == 0 and s % TQ == 0
    f32 = jnp.float32
    bf = jnp.bfloat16
    row = lambda g: g.reshape(1, -1).astype(f32)

    x_flat = x.reshape(t, d)
    pos_col = positions.reshape(t, 1).astype(f32)
    pos_row = positions.reshape(1, t).astype(f32)

    w_in_t = _bf16(w_in).T
    w_kr_t = jnp.zeros((HEAD_PAD, D_MODEL), bf)
    w_kr_t = w_kr_t.at[ROPE_LO:HALF_TILE].set(w_in_t[LATENT_DIM:LATENT_DIM + ROPE_HALF])
    w_kr_t = w_kr_t.at[HALF_TILE + ROPE_LO:].set(w_in_t[LATENT_DIM + ROPE_HALF:REST_START])
    w_lat_t = jnp.concatenate([w_in_t[:Q_LORA_RANK], w_kr_t, w_in_t[Q_LORA_RANK:LATENT_DIM]], axis=0)
    w_rest_t = w_in_t[REST_START:]
    compact = HEAD_SRC[HEAD_SRC >= 0]
    w_uq_t = _bf16(w_uq.reshape(Q_LORA_RANK, N_HEADS, QK_HEAD_DIM)[:, :, compact].reshape(Q_LORA_RANK, -1).T)
    w_uk_pad = _bf16(_to_head_tiles(w_uk, QK_NOPE_DIM))
    w_uv_t = _bf16(w_uv.T)
    head_gain = lambda g: _to_head_tiles(g.reshape(1, QK_HEAD_DIM).astype(f32), QK_HEAD_DIM)
    q_gain_cols = jnp.broadcast_to(q_head_norm.astype(f32)[compact].reshape(QK_HEAD_DIM, 1), (QK_HEAD_DIM, TM_MIX))
    inv_freq = 1.0 / (ROPE_THETA ** (jnp.arange(ROPE_HALF, dtype=f32) / ROPE_HALF))
    inv_freq_row = _to_head_tiles(
        jnp.concatenate([jnp.zeros((QK_NOPE_DIM,), f32), inv_freq, inv_freq]).reshape(1, QK_HEAD_DIM), QK_HEAD_DIM)
    inv_freq_cols = jnp.broadcast_to(inv_freq.reshape(ROPE_HALF, 1), (ROPE_HALF, TM_MIX))

    in_hbm = pl.BlockSpec(memory_space=pl.ANY)
    ffn_weight_scratch = [
        pltpu.VMEM((d, D_FF), bf), pltpu.VMEM((d, D_FF), bf), pltpu.VMEM((D_FF, d), bf),
        pltpu.VMEM((2, d // WEIGHT_CHUNKS, D_FF), f32), pltpu.VMEM((2, D_FF // WEIGHT_CHUNKS, d), f32),
        pltpu.SemaphoreType.DMA((2,))]

    x1 = pl.pallas_call(
        _ffn_kernel,
        grid=(t // TM_FFN1,),
        in_specs=[_rows(TM_FFN1, d), _resident()] + [in_hbm] * 3,
        out_specs=_rows(TM_FFN1, d),
        out_shape=jax.ShapeDtypeStruct((t, d), f32),
        scratch_shapes=[pltpu.VMEM((TM_FFN1, D_FF), bf)] + ffn_weight_scratch,
        compiler_params=_params(1),
        name="ffn1",
    )(x_flat, row(ffn1_norm), ffn1_w_gate, ffn1_w_up, ffn1_w_down)

    heads_w = N_HEADS * HEAD_PAD
    pair = 2 * HEAD_PAD
    n_pairs = N_HEADS // 2
    tile_major = pl.BlockSpec((None, heads_w, TM_MIX), lambda i: (i, 0, 0))
    q_t, k, v_t, gate_a, gated_yb = pl.pallas_call(
        functools.partial(_mix_kernel, s),
        grid=(t // TM_MIX,),
        in_specs=[_rows(TM_MIX, d), _rows(TM_MIX, 1), _cols(1, TM_MIX)] + [_resident()] * 15,
        out_specs=[tile_major, pl.BlockSpec((n_pairs, TM_MIX, pair), lambda i: (0, i, 0)), tile_major,
                   _rows(TM_MIX, d), _rows(TM_MIX, d)],
        out_shape=[jax.ShapeDtypeStruct((t // TM_MIX, heads_w, TM_MIX), bf), jax.ShapeDtypeStruct((n_pairs, t, pair), bf),
                   jax.ShapeDtypeStruct((t // TM_MIX, heads_w, TM_MIX), bf), jax.ShapeDtypeStruct((t, d), bf),
                   jax.ShapeDtypeStruct((t, d), bf)],
        scratch_shapes=[pltpu.VMEM((TM_MIX + 2 * SUBLANES, CONV_DIM), f32)],
        compiler_params=_params(1),
        name="mix",
    )(x1, pos_col, pos_row, inv_freq_row, inv_freq_cols, row(mix_norm), w_lat_t, w_rest_t, row(gate_bias), row(q_a_norm),
      w_uq_t, row(kv_a_norm), w_uk_pad, w_uv_t, q_gain_cols, head_gain(k_head_norm),
      conv_w.astype(f32), _bf16(w_proj_conv))

    pp = ATTN_PAIRS_PER_STEP
    feature_major = pl.BlockSpec((s // TM_MIX, pp * pair, TM_MIX), lambda bi, hp: (bi, hp, 0))
    attn = pl.pallas_call(
        _attn_kernel,
        grid=(b, n_pairs // pp),
        in_specs=[feature_major, pl.BlockSpec((pp, s, pair), lambda bi, hp: (hp, bi, 0)), feature_major],
        out_specs=pl.BlockSpec((pp, s, LANES), lambda bi, hp: (hp, bi, 0)),
        out_shape=jax.ShapeDtypeStruct((n_pairs, t, LANES), bf),
        compiler_params=_params(2),
        name="attn",
    )(q_t, k, v_t)

    out = pl.pallas_call(
        _post_kernel,
        grid=(t // TM_FFN,),
        in_specs=[_rows(TM_FFN, d), pl.BlockSpec((n_pairs, TM_FFN, LANES), lambda i: (0, i, 0)),
                  _rows(TM_FFN, d), _rows(TM_FFN, d)]
        + [_resident()] * 3 + [in_hbm] * 3,
        out_specs=_rows(TM_FFN, d),
        out_shape=jax.ShapeDtypeStruct((t, d), f32),
        scratch_shapes=[pltpu.VMEM((TM_FFN, D_FF), bf)] + ffn_weight_scratch,
        compiler_params=_params(1),
        name="post",
    )(x1, attn, gate_a, gated_yb, _bf16(w_proj_attn), _bf16(w_out), row(ffn2_norm),
      ffn2_w_gate, ffn2_w_up, ffn2_w_down)
    return out.reshape(b, s, d)
```

```python
import functools

import jax
import jax.numpy as jnp
import numpy as np
from jax import lax
from jax.experimental import pallas as pl
from jax.experimental.pallas import tpu as pltpu

D_MODEL = 1024
D_FF = 2816
N_HEADS = 8
QK_NOPE_DIM = 64
QK_ROPE_DIM = 32
QK_HEAD_DIM = QK_NOPE_DIM + QK_ROPE_DIM
V_HEAD_DIM = 64
Q_LORA_RANK = 384
KV_LORA_RANK = 256
CONV_DIM = 1024
CONV_WIDTH = 3
ROPE_THETA = 10000.0
NORM_EPS = 1e-6

LANES = 128
SUBLANES = 8
MXU_WIDTH = 256
CHUNK = MXU_WIDTH
HEAD_PAD = LANES
HALF_TILE = LANES // 2
ROPE_HALF = QK_ROPE_DIM // 2
ROPE_LO = HALF_TILE - ROPE_HALF


def _head_feature_sources():
    src = np.full((HEAD_PAD,), -1, np.int64)
    src[0:ROPE_LO] = np.arange(ROPE_LO)
    src[ROPE_LO:HALF_TILE] = QK_NOPE_DIM + np.arange(ROPE_HALF)
    rest = QK_NOPE_DIM - ROPE_LO
    src[HALF_TILE:HALF_TILE + rest] = ROPE_LO + np.arange(rest)
    src[HALF_TILE + ROPE_LO:HEAD_PAD] = QK_NOPE_DIM + ROPE_HALF + np.arange(ROPE_HALF)
    return src


HEAD_SRC = _head_feature_sources()

LATENT_DIM = Q_LORA_RANK + KV_LORA_RANK
REST_START = LATENT_DIM + QK_ROPE_DIM
OFF_XC = 0
OFF_GB = OFF_XC + CONV_DIM
OFF_GC = OFF_GB + CONV_DIM
OFF_GA_LOGIT = OFF_GC + CONV_DIM
OFF_GB_LOGIT = OFF_GA_LOGIT + D_MODEL

FF_CHUNKS = (768, 768, 768, 512)
VMEM_LIMIT = 56 * 1024 * 1024

TM_FFN = 512
TM_FFN1 = 1024
TM_MIX = 512
TQ = 256
ATTN_PAIRS_PER_STEP = 4
PROJ_AHEAD = 1
SCORE_LOOKAHEAD = 3
MASK_VALUE = -1e30
LOG2_E = 1.4426950408889634


def _bf16(x):
    return x.astype(jnp.bfloat16)


def _dot(a, b):
    return jnp.dot(a, b, preferred_element_type=jnp.float32)


def _dot_nt(a, b):
    return lax.dot_general(a, b, (((1,), (1,)), ((), ())), preferred_element_type=jnp.float32)


def _sum_sq(x):
    return jnp.sum(x * x, axis=-1, keepdims=True)


def _rmsnorm(x, gain, n):
    return x * lax.rsqrt(_sum_sq(x) * (1.0 / n) + NORM_EPS) * gain


def _sigmoid(x):
    return 0.5 * jnp.tanh(0.5 * x) + 0.5


def _swiglu_half_step(x, gain, wg_ref, wu_ref, wd_ref, act_ref):
    h = _bf16(_rmsnorm(x, gain, D_MODEL))
    c0 = 0
    for ck in FF_CHUNKS:
        g = _dot(h, wg_ref[:, c0:c0 + ck])
        u = _dot(h, wu_ref[:, c0:c0 + ck])
        act_ref[:, c0:c0 + ck] = _bf16(g * _sigmoid(g) * u)
        c0 += ck
    return x + 0.5 * _dot(act_ref[...], wd_ref[...])


def _ffn_kernel(x_ref, gain_ref, wg_ref, wu_ref, wd_ref, o_ref, act_ref):
    o_ref[...] = _swiglu_half_step(x_ref[...], gain_ref[...], wg_ref, wu_ref, wd_ref, act_ref)


def _swap_halves(t):
    return pltpu.roll(t, HALF_TILE, 1)


def _rope_tables_token_major(cos, sin):
    n = cos.shape[1]
    gap = HEAD_PAD - HALF_TILE - ROPE_HALF
    ones = lambda rows: jnp.ones((rows, n), jnp.float32)
    zeros = lambda rows: jnp.zeros((rows, n), jnp.float32)
    cos_f = jnp.concatenate([ones(ROPE_LO), cos, ones(gap), cos], axis=0)
    sin_f = jnp.concatenate([zeros(ROPE_LO), -sin, zeros(gap), sin], axis=0)
    return cos_f.T, sin_f.T


def _q_head_feature_major(x, gain, cos, sin):
    inv = lax.rsqrt(jnp.sum(x * x, axis=0, keepdims=True) * (1.0 / QK_HEAD_DIM) + NORM_EPS)
    y = x * gain
    lo2 = HALF_TILE + ROPE_HALF
    y1, y2 = y[ROPE_LO:HALF_TILE], y[lo2:QK_HEAD_DIM]
    pad = jnp.zeros((HEAD_PAD - QK_HEAD_DIM, x.shape[1]), jnp.float32)
    return jnp.concatenate([y[0:ROPE_LO] * inv, (y1 * cos - y2 * sin) * inv, y[HALF_TILE:lo2] * inv, pad,
                            (y2 * cos + y1 * sin) * inv], axis=0)


def _mix_kernel(seq_len, x_ref, pos_row_ref, invf_col_ref, mixg_ref, wlat_t_ref,
                wrest_t_ref, bias_ref, qag_ref, wuqt_ref, kvag_ref, wuk_ref, wuvt_ref, qhg_ref, khg_ref,
                convw_ref, wpc_ref, qt_ref, k_ref, vt_ref, ga_ref, gyb_ref, tail_ref):
    tm = x_ref.shape[0]
    h = _bf16(_rmsnorm(x_ref[...], mixg_ref[...], D_MODEL))

    def proj(off, width):
        return _dot_nt(h, wrest_t_ref[off:off + width, :])

    @pl.when((pl.program_id(0) * tm) % seq_len == 0)
    def _():
        tail_ref[...] = jnp.zeros((SUBLANES, CONV_DIM), jnp.float32)

    head_row = lax.broadcasted_iota(jnp.int32, (SUBLANES, CHUNK), 0)

    def shifted(u, tail, shift):
        rolled = pltpu.roll(u, shift, 0)
        first = jnp.where(head_row < shift, pltpu.roll(tail, shift, 0), rolled[0:SUBLANES])
        return jnp.concatenate([first, rolled[SUBLANES:]], axis=0)

    n_chunks = CONV_DIM // CHUNK
    heads_per_chunk = N_HEADS // n_chunks
    y_b = []
    gate_b = []

    def conv_projections(c):
        return tuple(proj(off + c * CHUNK, CHUNK) for off in (OFF_GC, OFF_XC, OFF_GB))

    def conv_chunk(c, projections):
        p_gc, p_xc, p_gb = projections
        cs = slice(c * CHUNK, (c + 1) * CHUNK)
        u = p_gc * p_xc
        tail = tail_ref[:, cs]
        z = (convw_ref[0:1, cs] * shifted(u, tail, 2) + convw_ref[1:2, cs] * shifted(u, tail, 1)
             + convw_ref[2:3, cs] * u)
        tail_ref[:, cs] = u[tm - SUBLANES:tm]
        y_b.append(_bf16(p_gb * z))

    def gate_projections(c):
        return tuple(proj(off + c * CHUNK, CHUNK) for off in (OFF_GA_LOGIT, OFF_GB_LOGIT))

    def gate_chunk(c, projections):
        p_ga, p_gbl = projections
        cs = slice(c * CHUNK, (c + 1) * CHUNK)
        ga_ref[:, cs] = _bf16(_sigmoid(p_ga + bias_ref[:, cs]))
        gate_b.append(_sigmoid(p_gbl + bias_ref[:, D_MODEL + cs.start:D_MODEL + cs.stop]))

    def q_heads(tokens, q_t):
        for hd in range(N_HEADS):
            qt_ref[hd * HEAD_PAD:(hd + 1) * HEAD_PAD, tokens] = _bf16(_q_head_feature_major(
                q_t[hd * QK_HEAD_DIM:(hd + 1) * QK_HEAD_DIM, :], q_gain[:, tokens], cos_q[:, tokens], sin_q[:, tokens]))

    def k_heads(c, k_nope):
        for i in range(heads_per_chunk):
            kn = k_nope[:, i * HEAD_PAD:(i + 1) * HEAD_PAD]
            inv = lax.rsqrt((_sum_sq(kn) + k_rope_sq) * (1.0 / QK_HEAD_DIM) + NORM_EPS)
            hd = c * heads_per_chunk + i
            k_ref[hd // 2, :, (hd % 2) * HEAD_PAD:(hd % 2 + 1) * HEAD_PAD] = _bf16((kn * k_gain + k_rope_rot) * inv)

    q_kr = _dot_nt(h, wlat_t_ref[0:Q_LORA_RANK + HEAD_PAD, :])
    q_lat = q_kr[:, 0:Q_LORA_RANK]
    k_rope = q_kr[:, Q_LORA_RANK:]
    kv_lat = _dot_nt(h, wlat_t_ref[Q_LORA_RANK + HEAD_PAD:, :])
    conv_p = [conv_projections(c) for c in range(PROJ_AHEAD)]

    ang_q = invf_col_ref[...] * pos_row_ref[...]
    cos_q, sin_q = jnp.cos(ang_q), jnp.sin(ang_q)
    q_gain = qhg_ref[...] * (QK_HEAD_DIM ** -0.5 * LOG2_E)
    cos_k, sin_k = _rope_tables_token_major(cos_q, sin_q)
    k_gain = khg_ref[...]
    k_rope_g = k_rope * k_gain
    k_rope_rot = k_rope_g * cos_k + _swap_halves(k_rope_g) * sin_k
    k_rope_sq = _sum_sq(k_rope)

    qn = _bf16(_rmsnorm(q_lat, qag_ref[...], Q_LORA_RANK))
    ckv = _bf16(_rmsnorm(kv_lat, kvag_ref[...], KV_LORA_RANK))
    gate_p = []
    half = tm // 2
    for c in range(n_chunks):
        if c % (n_chunks // 2) == 0:
            wave = c // (n_chunks // 2)
            tokens = slice(wave * half, (wave + 1) * half)
            q_heads(tokens, _dot_nt(wuqt_ref[...], qn[tokens]))
        k_heads(c, _dot(ckv, wuk_ref[:, c * CHUNK:(c + 1) * CHUNK]))
        if c + PROJ_AHEAD < n_chunks:
            conv_p.append(conv_projections(c + PROJ_AHEAD))
        gate_p.append(gate_projections(c))
    v_t = _bf16(_dot_nt(wuvt_ref[...], ckv))
    pad_rows = lax.broadcasted_iota(jnp.int32, (HEAD_PAD - V_HEAD_DIM, tm), 0)
    ones_block = jnp.where(pad_rows == 0, 1.0, 0.0).astype(jnp.bfloat16)
    for hd in range(N_HEADS):
        vt_ref[hd * HEAD_PAD:hd * HEAD_PAD + V_HEAD_DIM, :] = v_t[hd * V_HEAD_DIM:(hd + 1) * V_HEAD_DIM, :]
        vt_ref[hd * HEAD_PAD + V_HEAD_DIM:(hd + 1) * HEAD_PAD, :] = ones_block

    for c in range(n_chunks):
        conv_chunk(c, conv_p[c])
        gate_chunk(c, gate_p[c])
    y_conv = _dot(jnp.concatenate(y_b, axis=1), wpc_ref[...])
    for c, g in enumerate(gate_b):
        cs = slice(c * CHUNK, (c + 1) * CHUNK)
        gyb_ref[:, cs] = _bf16(g * y_conv[:, cs])


def _attn_kernel(qt_ref, k_ref, vt_ref, o_ref):
    n_pairs, s_len, _ = k_ref.shape
    key = lax.broadcasted_iota(jnp.int32, (TQ, TQ), 0)
    qry = lax.broadcasted_iota(jnp.int32, (TQ, TQ), 1)
    n_q = s_len // TQ
    tm = qt_ref.shape[2]

    def tokens(ref, hd, lo, hi):
        hs = slice(hd * HEAD_PAD, (hd + 1) * HEAD_PAD)
        parts = [ref[j, hs, max(lo - j * tm, 0):min(hi - j * tm, tm)] for j in range(lo // tm, -(-hi // tm))]
        return parts[0] if len(parts) == 1 else jnp.concatenate(parts, axis=1)

    def scores(qi, hd):
        lo, hi = qi * TQ, (qi + 1) * TQ
        k = k_ref[hd // 2, 0:hi, (hd % 2) * HEAD_PAD:(hd % 2 + 1) * HEAD_PAD]
        s = _dot(k, tokens(qt_ref, hd, lo, hi))
        s_diag = jnp.where(key <= qry, s[lo:hi, :], MASK_VALUE)
        s = s_diag if qi == 0 else jnp.concatenate([s[0:lo, :], s_diag], axis=0)
        return s, jnp.max(s, axis=0, keepdims=True)

    def finish(qi, hd, s_and_max):
        hi = (qi + 1) * TQ
        s, m = s_and_max
        p = jnp.exp2(s - m)
        acc = _dot(tokens(vt_ref, hd, 0, hi), _bf16(p))
        return acc[0:V_HEAD_DIM, :] / acc[V_HEAD_DIM:V_HEAD_DIM + 1, :]

    chains = [(qi, hd) for qi in reversed(range(n_q)) for hd in range(2 * n_pairs)]
    pending = [scores(*c) for c in chains[:SCORE_LOOKAHEAD]]
    outs = []
    for i, (qi, hd) in enumerate(chains):
        s = pending.pop(0)
        if i + SCORE_LOOKAHEAD < len(chains):
            pending.append(scores(*chains[i + SCORE_LOOKAHEAD]))
        outs.append(finish(qi, hd, s))
        if hd % 2 == 1:
            o_ref[hd // 2, qi * TQ:(qi + 1) * TQ, :] = _bf16(jnp.concatenate(outs, axis=0).T)
            outs = []


def _post_kernel(x_ref, attn_ref, ga_ref, gyb_ref, wpa_ref, wout_ref, gain_ref, wg_ref, wu_ref, wd_ref,
                 o_ref, act_ref):
    attn = jnp.concatenate([attn_ref[hp] for hp in range(attn_ref.shape[0])], axis=1)
    y_a = _dot(attn, wpa_ref[...])
    merged = ga_ref[...].astype(jnp.float32) * y_a + gyb_ref[...].astype(jnp.float32)
    x2 = x_ref[...] + _dot(_bf16(merged), wout_ref[...])
    o_ref[...] = _swiglu_half_step(x2, gain_ref[...], wg_ref, wu_ref, wd_ref, act_ref)


def _rows(tm, width):
    return pl.BlockSpec((tm, width), lambda i: (i, 0))


def _cols(height, tm):
    return pl.BlockSpec((height, tm), lambda i: (0, i))


def _resident():
    return pl.BlockSpec(memory_space=pltpu.VMEM)


def _params(n_axes):
    return pltpu.CompilerParams(dimension_semantics=("arbitrary",) * n_axes, vmem_limit_bytes=VMEM_LIMIT)


def _to_head_tiles(w, real):
    kdim = w.shape[0]
    w = w.reshape(kdim, -1, real)
    w = jnp.pad(w, ((0, 0), (0, 0), (0, 1)))
    src = np.where((HEAD_SRC >= 0) & (HEAD_SRC < real), HEAD_SRC, real)
    return w[:, :, src].reshape(kdim, -1)


def kernel(x, positions, ffn1_norm, ffn1_w_gate, ffn1_w_up, ffn1_w_down, mix_norm, w_in, gate_bias, q_a_norm, w_uq, kv_a_norm, w_uk, w_uv, q_head_norm, k_head_norm, w_proj_attn, conv_w, w_proj_conv, w_out, ffn2_norm, ffn2_w_gate, ffn2_w_up, ffn2_w_down):
    b, s, d = x.shape
    t = b * s
    assert d == D_MODEL and t % TM_FFN == 0 and s % TM_MIX == 0 and s % TQ == 0
    f32 = jnp.float32
    bf = jnp.bfloat16
    row = lambda g: g.reshape(1, -1).astype(f32)

    x_flat = x.reshape(t, d)
    pos_row = positions.reshape(1, t).astype(f32)

    w_in_t = _bf16(w_in).T
    w_kr_t = jnp.zeros((HEAD_PAD, D_MODEL), bf)
    w_kr_t = w_kr_t.at[ROPE_LO:HALF_TILE].set(w_in_t[LATENT_DIM:LATENT_DIM + ROPE_HALF])
    w_kr_t = w_kr_t.at[HALF_TILE + ROPE_LO:].set(w_in_t[LATENT_DIM + ROPE_HALF:REST_START])
    w_lat_t = jnp.concatenate([w_in_t[:Q_LORA_RANK], w_kr_t, w_in_t[Q_LORA_RANK:LATENT_DIM]], axis=0)
    w_rest_t = w_in_t[REST_START:]
    compact = HEAD_SRC[HEAD_SRC >= 0]
    w_uq_t = _bf16(w_uq.reshape(Q_LORA_RANK, N_HEADS, QK_HEAD_DIM)[:, :, compact].reshape(Q_LORA_RANK, -1).T)
    w_uk_pad = _bf16(_to_head_tiles(w_uk, QK_NOPE_DIM))
    w_uv_t = _bf16(w_uv.T)
    head_gain = lambda g: _to_head_tiles(g.reshape(1, QK_HEAD_DIM).astype(f32), QK_HEAD_DIM)
    q_gain_cols = jnp.broadcast_to(q_head_norm.astype(f32)[compact].reshape(QK_HEAD_DIM, 1), (QK_HEAD_DIM, TM_MIX))
    inv_freq = 1.0 / (ROPE_THETA ** (jnp.arange(ROPE_HALF, dtype=f32) / ROPE_HALF))
    inv_freq_cols = jnp.broadcast_to(inv_freq.reshape(ROPE_HALF, 1), (ROPE_HALF, TM_MIX))

    def ffn_weights(wg, wu, wd):
        return _bf16(wg), _bf16(wu), _bf16(wd)

    x1 = pl.pallas_call(
        _ffn_kernel,
        grid=(t // TM_FFN1,),
        in_specs=[_rows(TM_FFN1, d)] + [_resident()] * 4,
        out_specs=_rows(TM_FFN1, d),
        out_shape=jax.ShapeDtypeStruct((t, d), f32),
        scratch_shapes=[pltpu.VMEM((TM_FFN1, D_FF), bf)],
        compiler_params=_params(1),
        name="ffn1",
    )(x_flat, row(ffn1_norm), *ffn_weights(ffn1_w_gate, ffn1_w_up, ffn1_w_down))

    heads_w = N_HEADS * HEAD_PAD
    pair = 2 * HEAD_PAD
    n_pairs = N_HEADS // 2
    tile_major = pl.BlockSpec((None, heads_w, TM_MIX), lambda i: (i, 0, 0))
    q_t, k, v_t, gate_a, gated_yb = pl.pallas_call(
        functools.partial(_mix_kernel, s),
        grid=(t // TM_MIX,),
        in_specs=[_rows(TM_MIX, d), _cols(1, TM_MIX)] + [_resident()] * 14,
        out_specs=[tile_major, pl.BlockSpec((n_pairs, TM_MIX, pair), lambda i: (0, i, 0)), tile_major,
                   _rows(TM_MIX, d), _rows(TM_MIX, d)],
        out_shape=[jax.ShapeDtypeStruct((t // TM_MIX, heads_w, TM_MIX), bf), jax.ShapeDtypeStruct((n_pairs, t, pair), bf),
                   jax.ShapeDtypeStruct((t // TM_MIX, heads_w, TM_MIX), bf), jax.ShapeDtypeStruct((t, d), bf),
                   jax.ShapeDtypeStruct((t, d), bf)],
        scratch_shapes=[pltpu.VMEM((SUBLANES, CONV_DIM), f32)],
        compiler_params=_params(1),
        name="mix",
    )(x1, pos_row, inv_freq_cols, row(mix_norm), w_lat_t, w_rest_t, row(gate_bias), row(q_a_norm),
      w_uq_t, row(kv_a_norm), w_uk_pad, w_uv_t, q_gain_cols, head_gain(k_head_norm),
      conv_w.astype(f32), _bf16(w_proj_conv))

    pp = ATTN_PAIRS_PER_STEP
    feature_major = pl.BlockSpec((s // TM_MIX, pp * pair, TM_MIX), lambda bi, hp: (bi, hp, 0))
    attn = pl.pallas_call(
        _attn_kernel,
        grid=(b, n_pairs // pp),
        in_specs=[feature_major, pl.BlockSpec((pp, s, pair), lambda bi, hp: (hp, bi, 0)), feature_major],
        out_specs=pl.BlockSpec((pp, s, LANES), lambda bi, hp: (hp, bi, 0)),
        out_shape=jax.ShapeDtypeStruct((n_pairs, t, LANES), bf),
        compiler_params=_params(2),
        name="attn",
    )(q_t, k, v_t)

    out = pl.pallas_call(
        _post_kernel,
        grid=(t // TM_FFN,),
        in_specs=[_rows(TM_FFN, d), pl.BlockSpec((n_pairs, TM_FFN, LANES), lambda i: (0, i, 0)),
                  _rows(TM_FFN, d), _rows(TM_FFN, d)]
        + [_resident()] * 6,
        out_specs=_rows(TM_FFN, d),
        out_shape=jax.ShapeDtypeStruct((t, d), f32),
        scratch_shapes=[pltpu.VMEM((TM_FFN, D_FF), bf)],
        compiler_params=_params(1),
        name="post",
    )(x1, attn, gate_a, gated_yb, _bf16(w_proj_attn), _bf16(w_out), row(ffn2_norm),
      *ffn_weights(ffn2_w_gate, ffn2_w_up, ffn2_w_down))
    return out.reshape(b, s, d)
```

```python
import functools

import jax
import jax.numpy as jnp
import numpy as np
from jax import lax
from jax.experimental import pallas as pl
from jax.experimental.pallas import tpu as pltpu

D_MODEL = 1024
D_FF = 2816
N_HEADS = 8
QK_NOPE_DIM = 64
QK_ROPE_DIM = 32
QK_HEAD_DIM = QK_NOPE_DIM + QK_ROPE_DIM
V_HEAD_DIM = 64
Q_LORA_RANK = 384
KV_LORA_RANK = 256
CONV_DIM = 1024
CONV_WIDTH = 3
ROPE_THETA = 10000.0
NORM_EPS = 1e-6

LANES = 128
SUBLANES = 8
MXU_WIDTH = 256
CHUNK = MXU_WIDTH
HEAD_PAD = LANES
HALF_TILE = LANES // 2
ROPE_HALF = QK_ROPE_DIM // 2
ROPE_LO = HALF_TILE - ROPE_HALF


def _head_feature_sources():
    src = np.full((HEAD_PAD,), -1, np.int64)
    src[0:ROPE_LO] = np.arange(ROPE_LO)
    src[ROPE_LO:HALF_TILE] = QK_NOPE_DIM + np.arange(ROPE_HALF)
    rest = QK_NOPE_DIM - ROPE_LO
    src[HALF_TILE:HALF_TILE + rest] = ROPE_LO + np.arange(rest)
    src[HALF_TILE + ROPE_LO:HEAD_PAD] = QK_NOPE_DIM + ROPE_HALF + np.arange(ROPE_HALF)
    return src


HEAD_SRC = _head_feature_sources()

LATENT_DIM = Q_LORA_RANK + KV_LORA_RANK
REST_START = LATENT_DIM + QK_ROPE_DIM
OFF_XC = 0
OFF_GB = OFF_XC + CONV_DIM
OFF_GC = OFF_GB + CONV_DIM
OFF_GA_LOGIT = OFF_GC + CONV_DIM
OFF_GB_LOGIT = OFF_GA_LOGIT + D_MODEL

FF_CHUNKS = (768, 768, 768, 512)
VMEM_LIMIT = 56 * 1024 * 1024

TM_FFN = 512
TM_FFN1 = 1024
TM_MIX = 512
TQ = 256
ATTN_PAIRS_PER_STEP = 4
PROJ_AHEAD = 1
SCORE_LOOKAHEAD = 3
MASK_VALUE = -1e30
LOG2_E = 1.4426950408889634


def _bf16(x):
    return x.astype(jnp.bfloat16)


def _dot(a, b):
    return jnp.dot(a, b, preferred_element_type=jnp.float32)


def _dot_nt(a, b):
    return lax.dot_general(a, b, (((1,), (1,)), ((), ())), preferred_element_type=jnp.float32)


def _sum_sq(x):
    return jnp.sum(x * x, axis=-1, keepdims=True)


def _rmsnorm(x, gain, n):
    return x * lax.rsqrt(_sum_sq(x) * (1.0 / n) + NORM_EPS) * gain


def _sigmoid(x):
    return 0.5 * jnp.tanh(0.5 * x) + 0.5


def _swiglu_half_step(x, gain, wg_ref, wu_ref, wd_ref, act_ref):
    h = _bf16(_rmsnorm(x, gain, D_MODEL))
    c0 = 0
    for ck in FF_CHUNKS:
        g = _dot(h, wg_ref[:, c0:c0 + ck])
        u = _dot(h, wu_ref[:, c0:c0 + ck])
        act_ref[:, c0:c0 + ck] = _bf16(g * _sigmoid(g) * u)
        c0 += ck
    return x + 0.5 * _dot(act_ref[...], wd_ref[...])


def _ffn_kernel(x_ref, gain_ref, wg_ref, wu_ref, wd_ref, o_ref, act_ref):
    o_ref[...] = _swiglu_half_step(x_ref[...], gain_ref[...], wg_ref, wu_ref, wd_ref, act_ref)


def _swap_halves(t):
    return pltpu.roll(t, HALF_TILE, 1)


def _rope_tables_token_major(cos, sin):
    n = cos.shape[1]
    gap = HEAD_PAD - HALF_TILE - ROPE_HALF
    ones = lambda rows: jnp.ones((rows, n), jnp.float32)
    zeros = lambda rows: jnp.zeros((rows, n), jnp.float32)
    cos_f = jnp.concatenate([ones(ROPE_LO), cos, ones(gap), cos], axis=0)
    sin_f = jnp.concatenate([zeros(ROPE_LO), -sin, zeros(gap), sin], axis=0)
    return cos_f.T, sin_f.T


def _q_head_feature_major(x, gain, cos, sin):
    inv = lax.rsqrt(jnp.sum(x * x, axis=0, keepdims=True) * (1.0 / QK_HEAD_DIM) + NORM_EPS)
    y = x * gain
    lo2 = HALF_TILE + ROPE_HALF
    y1, y2 = y[ROPE_LO:HALF_TILE], y[lo2:QK_HEAD_DIM]
    pad = jnp.zeros((HEAD_PAD - QK_HEAD_DIM, x.shape[1]), jnp.float32)
    return jnp.concatenate([y[0:ROPE_LO] * inv, (y1 * cos - y2 * sin) * inv, y[HALF_TILE:lo2] * inv, pad,
                            (y2 * cos + y1 * sin) * inv], axis=0)


def _mix_kernel(seq_len, x_ref, pos_row_ref, invf_col_ref, mixg_ref, wlat_t_ref,
                wrest_t_ref, bias_ref, qag_ref, wuqt_ref, kvag_ref, wuk_ref, wuvt_ref, qhg_ref, khg_ref,
                convw_ref, wpc_ref, qt_ref, k_ref, vt_ref, ga_ref, gyb_ref, tail_ref):
    tm = x_ref.shape[0]
    h = _bf16(_rmsnorm(x_ref[...], mixg_ref[...], D_MODEL))

    def proj(off, width):
        return _dot_nt(h, wrest_t_ref[off:off + width, :])

    @pl.when((pl.program_id(0) * tm) % seq_len == 0)
    def _():
        tail_ref[...] = jnp.zeros((SUBLANES, CONV_DIM), jnp.float32)

    head_row = lax.broadcasted_iota(jnp.int32, (SUBLANES, CHUNK), 0)

    def shifted(u, tail, shift):
        rolled = pltpu.roll(u, shift, 0)
        first = jnp.where(head_row < shift, pltpu.roll(tail, shift, 0), rolled[0:SUBLANES])
        return jnp.concatenate([first, rolled[SUBLANES:]], axis=0)

    n_chunks = CONV_DIM // CHUNK
    heads_per_chunk = N_HEADS // n_chunks
    y_b = []
    gate_b = []

    def conv_projections(c):
        return tuple(proj(off + c * CHUNK, CHUNK) for off in (OFF_GC, OFF_XC, OFF_GB))

    def conv_chunk(c, projections):
        p_gc, p_xc, p_gb = projections
        cs = slice(c * CHUNK, (c + 1) * CHUNK)
        u = p_gc * p_xc
        tail = tail_ref[:, cs]
        z = (convw_ref[0:1, cs] * shifted(u, tail, 2) + convw_ref[1:2, cs] * shifted(u, tail, 1)
             + convw_ref[2:3, cs] * u)
        tail_ref[:, cs] = u[tm - SUBLANES:tm]
        y_b.append(_bf16(p_gb * z))

    def gate_projections(c):
        return tuple(proj(off + c * CHUNK, CHUNK) for off in (OFF_GA_LOGIT, OFF_GB_LOGIT))

    def gate_chunk(c, projections):
        p_ga, p_gbl = projections
        cs = slice(c * CHUNK, (c + 1) * CHUNK)
        ga_ref[:, cs] = _bf16(_sigmoid(p_ga + bias_ref[:, cs]))
        gate_b.append(_sigmoid(p_gbl + bias_ref[:, D_MODEL + cs.start:D_MODEL + cs.stop]))

    def q_heads(tokens, q_t):
        for hd in range(N_HEADS):
            qt_ref[hd * HEAD_PAD:(hd + 1) * HEAD_PAD, tokens] = _bf16(_q_head_feature_major(
                q_t[hd * QK_HEAD_DIM:(hd + 1) * QK_HEAD_DIM, :], q_gain[:, tokens], cos_q[:, tokens], sin_q[:, tokens]))

    def k_heads(c, k_nope):
        for i in range(heads_per_chunk):
            kn = k_nope[:, i * HEAD_PAD:(i + 1) * HEAD_PAD]
            inv = lax.rsqrt((_sum_sq(kn) + k_rope_sq) * (1.0 / QK_HEAD_DIM) + NORM_EPS)
            hd = c * heads_per_chunk + i
            k_ref[hd // 2, :, (hd % 2) * HEAD_PAD:(hd % 2 + 1) * HEAD_PAD] = _bf16((kn * k_gain + k_rope_rot) * inv)

    q_kr = _dot_nt(h, wlat_t_ref[0:Q_LORA_RANK + HEAD_PAD, :])
    q_lat = q_kr[:, 0:Q_LORA_RANK]
    k_rope = q_kr[:, Q_LORA_RANK:]
    kv_lat = _dot_nt(h, wlat_t_ref[Q_LORA_RANK + HEAD_PAD:, :])
    conv_p = [conv_projections(c) for c in range(PROJ_AHEAD)]

    ang_q = invf_col_ref[...] * pos_row_ref[...]
    cos_q, sin_q = jnp.cos(ang_q), jnp.sin(ang_q)
    q_gain = qhg_ref[...] * (QK_HEAD_DIM ** -0.5 * LOG2_E)
    cos_k, sin_k = _rope_tables_token_major(cos_q, sin_q)
    k_gain = khg_ref[...]
    k_rope_g = k_rope * k_gain
    k_rope_rot = k_rope_g * cos_k + _swap_halves(k_rope_g) * sin_k
    k_rope_sq = _sum_sq(k_rope)

    qn = _bf16(_rmsnorm(q_lat, qag_ref[...], Q_LORA_RANK))
    ckv = _bf16(_rmsnorm(kv_lat, kvag_ref[...], KV_LORA_RANK))
    gate_p = []
    half = tm // 2
    for c in range(n_chunks):
        if c % (n_chunks // 2) == 0:
            wave = c // (n_chunks // 2)
            tokens = slice(wave * half, (wave + 1) * half)
            q_heads(tokens, _dot_nt(wuqt_ref[...], qn[tokens]))
        k_heads(c, _dot(ckv, wuk_ref[:, c * CHUNK:(c + 1) * CHUNK]))
        if c + PROJ_AHEAD < n_chunks:
            conv_p.append(conv_projections(c + PROJ_AHEAD))
        gate_p.append(gate_projections(c))
    v_t = _bf16(_dot_nt(wuvt_ref[...], ckv))
    pad_rows = lax.broadcasted_iota(jnp.int32, (HEAD_PAD - V_HEAD_DIM, tm), 0)
    ones_block = jnp.where(pad_rows == 0, 1.0, 0.0).astype(jnp.bfloat16)
    for hd in range(N_HEADS):
        vt_ref[hd * HEAD_PAD:hd * HEAD_PAD + V_HEAD_DIM, :] = v_t[hd * V_HEAD_DIM:(hd + 1) * V_HEAD_DIM, :]
        vt_ref[hd * HEAD_PAD + V_HEAD_DIM:(hd + 1) * HEAD_PAD, :] = ones_block

    for c in range(n_chunks):
        conv_chunk(c, conv_p[c])
        gate_chunk(c, gate_p[c])
    y_conv = _dot(jnp.concatenate(y_b, axis=1), wpc_ref[...])
    for c, g in enumerate(gate_b):
        cs = slice(c * CHUNK, (c + 1) * CHUNK)
        gyb_ref[:, cs] = _bf16(g * y_conv[:, cs])


def _attn_kernel(qt_ref, k_ref, vt_ref, o_ref):
    n_pairs, s_len, _ = k_ref.shape
    key = lax.broadcasted_iota(jnp.int32, (TQ, TQ), 0)
    qry = lax.broadcasted_iota(jnp.int32, (TQ, TQ), 1)
    n_q = s_len // TQ
    tm = qt_ref.shape[2]

    def tokens(ref, hd, lo, hi):
        hs = slice(hd * HEAD_PAD, (hd + 1) * HEAD_PAD)
        parts = [ref[j, hs, max(lo - j * tm, 0):min(hi - j * tm, tm)] for j in range(lo // tm, -(-hi // tm))]
        return parts[0] if len(parts) == 1 else jnp.concatenate(parts, axis=1)

    def scores(qi, hd):
        lo, hi = qi * TQ, (qi + 1) * TQ
        k = k_ref[hd // 2, 0:hi, (hd % 2) * HEAD_PAD:(hd % 2 + 1) * HEAD_PAD]
        s = _dot(k, tokens(qt_ref, hd, lo, hi))
        s_diag = jnp.where(key <= qry, s[lo:hi, :], MASK_VALUE)
        s = s_diag if qi == 0 else jnp.concatenate([s[0:lo, :], s_diag], axis=0)
        return s, jnp.max(s, axis=0, keepdims=True)

    def finish(qi, hd, s_and_max):
        hi = (qi + 1) * TQ
        s, m = s_and_max
        p = jnp.exp2(_bf16(s - m))
        acc = _dot(tokens(vt_ref, hd, 0, hi), p)
        return acc[0:V_HEAD_DIM, :] / acc[V_HEAD_DIM:V_HEAD_DIM + 1, :]

    chains = [(qi, hd) for qi in reversed(range(n_q)) for hd in range(2 * n_pairs)]
    pending = [scores(*c) for c in chains[:SCORE_LOOKAHEAD]]
    outs = []
    for i, (qi, hd) in enumerate(chains):
        s = pending.pop(0)
        if i + SCORE_LOOKAHEAD < len(chains):
            pending.append(scores(*chains[i + SCORE_LOOKAHEAD]))
        outs.append(finish(qi, hd, s))
        if hd % 2 == 1:
            o_ref[hd // 2, qi * TQ:(qi + 1) * TQ, :] = _bf16(jnp.concatenate(outs, axis=0).T)
            outs = []


def _post_kernel(x_ref, attn_ref, ga_ref, gyb_ref, wpa_ref, wout_ref, gain_ref, wg_ref, wu_ref, wd_ref,
                 o_ref, act_ref):
    attn = jnp.concatenate([attn_ref[hp] for hp in range(attn_ref.shape[0])], axis=1)
    y_a = _dot(attn, wpa_ref[...])
    merged = ga_ref[...].astype(jnp.float32) * y_a + gyb_ref[...].astype(jnp.float32)
    x2 = x_ref[...] + _dot(_bf16(merged), wout_ref[...])
    o_ref[...] = _swiglu_half_step(x2, gain_ref[...], wg_ref, wu_ref, wd_ref, act_ref)


def _rows(tm, width):
    return pl.BlockSpec((tm, width), lambda i: (i, 0))


def _cols(height, tm):
    return pl.BlockSpec((height, tm), lambda i: (0, i))


def _resident():
    return pl.BlockSpec(memory_space=pltpu.VMEM)


def _params(n_axes):
    return pltpu.CompilerParams(dimension_semantics=("arbitrary",) * n_axes, vmem_limit_bytes=VMEM_LIMIT)


def _to_head_tiles(w, real):
    kdim = w.shape[0]
    w = w.reshape(kdim, -1, real)
    w = jnp.pad(w, ((0, 0), (0, 0), (0, 1)))
    src = np.where((HEAD_SRC >= 0) & (HEAD_SRC < real), HEAD_SRC, real)
    return w[:, :, src].reshape(kdim, -1)


def kernel(x, positions, ffn1_norm, ffn1_w_gate, ffn1_w_up, ffn1_w_down, mix_norm, w_in, gate_bias, q_a_norm, w_uq, kv_a_norm, w_uk, w_uv, q_head_norm, k_head_norm, w_proj_attn, conv_w, w_proj_conv, w_out, ffn2_norm, ffn2_w_gate, ffn2_w_up, ffn2_w_down):
    b, s, d = x.shape
    t = b * s
    assert d == D_MODEL and t % TM_FFN == 0 and s % TM_MIX == 0 and s % TQ == 0
    f32 = jnp.float32
    bf = jnp.bfloat16
    row = lambda g: g.reshape(1, -1).astype(f32)

    x_flat = x.reshape(t, d)
    pos_row = positions.reshape(1, t).astype(f32)

    w_in_t = _bf16(w_in).T
    w_kr_t = jnp.zeros((HEAD_PAD, D_MODEL), bf)
    w_kr_t = w_kr_t.at[ROPE_LO:HALF_TILE].set(w_in_t[LATENT_DIM:LATENT_DIM + ROPE_HALF])
    w_kr_t = w_kr_t.at[HALF_TILE + ROPE_LO:].set(w_in_t[LATENT_DIM + ROPE_HALF:REST_START])
    w_lat_t = jnp.concatenate([w_in_t[:Q_LORA_RANK], w_kr_t, w_in_t[Q_LORA_RANK:LATENT_DIM]], axis=0)
    w_rest_t = w_in_t[REST_START:]
    compact = HEAD_SRC[HEAD_SRC >= 0]
    w_uq_t = _bf16(w_uq.reshape(Q_LORA_RANK, N_HEADS, QK_HEAD_DIM)[:, :, compact].reshape(Q_LORA_RANK, -1).T)
    w_uk_pad = _bf16(_to_head_tiles(w_uk, QK_NOPE_DIM))
    w_uv_t = _bf16(w_uv.T)
    head_gain = lambda g: _to_head_tiles(g.reshape(1, QK_HEAD_DIM).astype(f32), QK_HEAD_DIM)
    q_gain_cols = jnp.broadcast_to(q_head_norm.astype(f32)[compact].reshape(QK_HEAD_DIM, 1), (QK_HEAD_DIM, TM_MIX))
    inv_freq = 1.0 / (ROPE_THETA ** (jnp.arange(ROPE_HALF, dtype=f32) / ROPE_HALF))
    inv_freq_cols = jnp.broadcast_to(inv_freq.reshape(ROPE_HALF, 1), (ROPE_HALF, TM_MIX))

    def ffn_weights(wg, wu, wd):
        return _bf16(wg), _bf16(wu), _bf16(wd)

    x1 = pl.pallas_call(
        _ffn_kernel,
        grid=(t // TM_FFN1,),
        in_specs=[_rows(TM_FFN1, d)] + [_resident()] * 4,
        out_specs=_rows(TM_FFN1, d),
        out_shape=jax.ShapeDtypeStruct((t, d), f32),
        scratch_shapes=[pltpu.VMEM((TM_FFN1, D_FF), bf)],
        compiler_params=_params(1),
        name="ffn1",
    )(x_flat, row(ffn1_norm), *ffn_weights(ffn1_w_gate, ffn1_w_up, ffn1_w_down))

    heads_w = N_HEADS * HEAD_PAD
    pair = 2 * HEAD_PAD
    n_pairs = N_HEADS // 2
    tile_major = pl.BlockSpec((None, heads_w, TM_MIX), lambda i: (i, 0, 0))
    q_t, k, v_t, gate_a, gated_yb = pl.pallas_call(
        functools.partial(_mix_kernel, s),
        grid=(t // TM_MIX,),
        in_specs=[_rows(TM_MIX, d), _cols(1, TM_MIX)] + [_resident()] * 14,
        out_specs=[tile_major, pl.BlockSpec((n_pairs, TM_MIX, pair), lambda i: (0, i, 0)), tile_major,
                   _rows(TM_MIX, d), _rows(TM_MIX, d)],
        out_shape=[jax.ShapeDtypeStruct((t // TM_MIX, heads_w, TM_MIX), bf), jax.ShapeDtypeStruct((n_pairs, t, pair), bf),
                   jax.ShapeDtypeStruct((t // TM_MIX, heads_w, TM_MIX), bf), jax.ShapeDtypeStruct((t, d), bf),
                   jax.ShapeDtypeStruct((t, d), bf)],
        scratch_shapes=[pltpu.VMEM((SUBLANES, CONV_DIM), f32)],
        compiler_params=_params(1),
        name="mix",
    )(x1, pos_row, inv_freq_cols, row(mix_norm), w_lat_t, w_rest_t, row(gate_bias), row(q_a_norm),
      w_uq_t, row(kv_a_norm), w_uk_pad, w_uv_t, q_gain_cols, head_gain(k_head_norm),
      conv_w.astype(f32), _bf16(w_proj_conv))

    pp = ATTN_PAIRS_PER_STEP
    feature_major = pl.BlockSpec((s // TM_MIX, pp * pair, TM_MIX), lambda bi, hp: (bi, hp, 0))
    attn = pl.pallas_call(
        _attn_kernel,
        grid=(b, n_pairs // pp),
        in_specs=[feature_major, pl.BlockSpec((pp, s, pair), lambda bi, hp: (hp, bi, 0)), feature_major],
        out_specs=pl.BlockSpec((pp, s, LANES), lambda bi, hp: (hp, bi, 0)),
        out_shape=jax.ShapeDtypeStruct((n_pairs, t, LANES), bf),
        compiler_params=_params(2),
        name="attn",
    )(q_t, k, v_t)

    out = pl.pallas_call(
        _post_kernel,
        grid=(t // TM_FFN,),
        in_specs=[_rows(TM_FFN, d), pl.BlockSpec((n_pairs, TM_FFN, LANES), lambda i: (0, i, 0)),
                  _rows(TM_FFN, d), _rows(TM_FFN, d)]
        + [_resident()] * 6,
        out_specs=_rows(TM_FFN, d),
        out_shape=jax.ShapeDtypeStruct((t, d), f32),
        scratch_shapes=[pltpu.VMEM((TM_FFN, D_FF), bf)],
        compiler_params=_params(1),
        name="post",
    )(x1, attn, gate_a, gated_yb, _bf16(w_proj_attn), _bf16(w_out), row(ffn2_norm),
      *ffn_weights(ffn2_w_gate, ffn2_w_up, ffn2_w_down))
    return out.reshape(b, s, d)
```

```python
import functools

import jax
import jax.numpy as jnp
import numpy as np
from jax import lax
from jax.experimental import pallas as pl
from jax.experimental.pallas import tpu as pltpu

D_MODEL = 1024
D_FF = 2816
N_HEADS = 8
QK_NOPE_DIM = 64
QK_ROPE_DIM = 32
QK_HEAD_DIM = QK_NOPE_DIM + QK_ROPE_DIM
V_HEAD_DIM = 64
Q_LORA_RANK = 384
KV_LORA_RANK = 256
CONV_DIM = 1024
CONV_WIDTH = 3
ROPE_THETA = 10000.0
NORM_EPS = 1e-6

LANES = 128
SUBLANES = 8
MXU_WIDTH = 256
CHUNK = MXU_WIDTH
HEAD_PAD = LANES
BF16_ROWS = 16
V_TILE = 64 + BF16_ROWS
HALF_TILE = LANES // 2
ROPE_HALF = QK_ROPE_DIM // 2
ROPE_LO = HALF_TILE - ROPE_HALF


def _head_feature_sources():
    src = np.full((HEAD_PAD,), -1, np.int64)
    src[0:ROPE_LO] = np.arange(ROPE_LO)
    src[ROPE_LO:HALF_TILE] = QK_NOPE_DIM + np.arange(ROPE_HALF)
    rest = QK_NOPE_DIM - ROPE_LO
    src[HALF_TILE:HALF_TILE + rest] = ROPE_LO + np.arange(rest)
    src[HALF_TILE + ROPE_LO:HEAD_PAD] = QK_NOPE_DIM + ROPE_HALF + np.arange(ROPE_HALF)
    return src


HEAD_SRC = _head_feature_sources()

LATENT_DIM = Q_LORA_RANK + KV_LORA_RANK
REST_START = LATENT_DIM + QK_ROPE_DIM
OFF_XC = 0
OFF_GB = OFF_XC + CONV_DIM
OFF_GC = OFF_GB + CONV_DIM
OFF_GA_LOGIT = OFF_GC + CONV_DIM
OFF_GB_LOGIT = OFF_GA_LOGIT + D_MODEL

FF_CHUNKS = (768, 768, 768, 512)
VMEM_LIMIT = 56 * 1024 * 1024

TM_FFN = 512
TM_FFN1 = 1024
TM_MIX = 512
TQ = 256
ATTN_PAIRS_PER_STEP = 4
PROJ_AHEAD = 1
SCORE_LOOKAHEAD = 3
MASK_VALUE = -1e30
LOG2_E = 1.4426950408889634


def _bf16(x):
    return x.astype(jnp.bfloat16)


def _dot(a, b):
    return jnp.dot(a, b, preferred_element_type=jnp.float32)


def _dot_nt(a, b):
    return lax.dot_general(a, b, (((1,), (1,)), ((), ())), preferred_element_type=jnp.float32)


def _sum_sq(x):
    return jnp.sum(x * x, axis=-1, keepdims=True)


def _rmsnorm(x, gain, n):
    return x * lax.rsqrt(_sum_sq(x) * (1.0 / n) + NORM_EPS) * gain


def _sigmoid(x):
    return 0.5 * jnp.tanh(0.5 * x) + 0.5


def _swiglu_half_step(x, gain, wg_ref, wu_ref, wd_ref, act_ref):
    h = _bf16(_rmsnorm(x, gain, D_MODEL))
    c0 = 0
    for ck in FF_CHUNKS:
        g = _dot(h, wg_ref[:, c0:c0 + ck])
        u = _dot(h, wu_ref[:, c0:c0 + ck])
        act_ref[:, c0:c0 + ck] = _bf16(g * _sigmoid(g) * u)
        c0 += ck
    return x + 0.5 * _dot(act_ref[...], wd_ref[...])


def _ffn_kernel(x_ref, gain_ref, wg_ref, wu_ref, wd_ref, o_ref, act_ref):
    o_ref[...] = _swiglu_half_step(x_ref[...], gain_ref[...], wg_ref, wu_ref, wd_ref, act_ref)


def _swap_halves(t):
    return pltpu.roll(t, HALF_TILE, 1)


def _rope_tables_token_major(cos, sin):
    n = cos.shape[1]
    gap = HEAD_PAD - HALF_TILE - ROPE_HALF
    ones = lambda rows: jnp.ones((rows, n), jnp.float32)
    zeros = lambda rows: jnp.zeros((rows, n), jnp.float32)
    cos_f = jnp.concatenate([ones(ROPE_LO), cos, ones(gap), cos], axis=0)
    sin_f = jnp.concatenate([zeros(ROPE_LO), -sin, zeros(gap), sin], axis=0)
    return cos_f.T, sin_f.T


def _q_head_feature_major(x, gain, cos, sin):
    inv = lax.rsqrt(jnp.sum(x * x, axis=0, keepdims=True) * (1.0 / QK_HEAD_DIM) + NORM_EPS)
    y = x * gain
    lo2 = HALF_TILE + ROPE_HALF
    y1, y2 = y[ROPE_LO:HALF_TILE], y[lo2:QK_HEAD_DIM]
    pad = jnp.zeros((HEAD_PAD - QK_HEAD_DIM, x.shape[1]), jnp.float32)
    return jnp.concatenate([y[0:ROPE_LO] * inv, (y1 * cos - y2 * sin) * inv, y[HALF_TILE:lo2] * inv, pad,
                            (y2 * cos + y1 * sin) * inv], axis=0)


def _mix_kernel(seq_len, x_ref, pos_row_ref, invf_col_ref, mixg_ref, wlat_t_ref,
                wrest_t_ref, bias_ref, qag_ref, wuqt_ref, kvag_ref, wuk_ref, wuvt_ref, qhg_ref, khg_ref,
                convw_ref, wpc_ref, qt_ref, k_ref, vt_ref, ga_ref, gyb_ref, tail_ref):
    tm = x_ref.shape[0]
    h = _bf16(_rmsnorm(x_ref[...], mixg_ref[...], D_MODEL))

    def proj(off, width):
        return _dot_nt(h, wrest_t_ref[off:off + width, :])

    @pl.when((pl.program_id(0) * tm) % seq_len == 0)
    def _():
        tail_ref[...] = jnp.zeros((SUBLANES, CONV_DIM), jnp.float32)

    head_row = lax.broadcasted_iota(jnp.int32, (SUBLANES, CHUNK), 0)

    def shifted(u, tail, shift):
        rolled = pltpu.roll(u, shift, 0)
        first = jnp.where(head_row < shift, pltpu.roll(tail, shift, 0), rolled[0:SUBLANES])
        return jnp.concatenate([first, rolled[SUBLANES:]], axis=0)

    n_chunks = CONV_DIM // CHUNK
    heads_per_chunk = N_HEADS // n_chunks
    y_b = []
    gate_b = []

    def conv_projections(c):
        return tuple(proj(off + c * CHUNK, CHUNK) for off in (OFF_GC, OFF_XC, OFF_GB))

    def conv_chunk(c, projections):
        p_gc, p_xc, p_gb = projections
        cs = slice(c * CHUNK, (c + 1) * CHUNK)
        u = p_gc * p_xc
        tail = tail_ref[:, cs]
        z = (convw_ref[0:1, cs] * shifted(u, tail, 2) + convw_ref[1:2, cs] * shifted(u, tail, 1)
             + convw_ref[2:3, cs] * u)
        tail_ref[:, cs] = u[tm - SUBLANES:tm]
        y_b.append(_bf16(p_gb * z))

    def gate_projections(c):
        return tuple(proj(off + c * CHUNK, CHUNK) for off in (OFF_GA_LOGIT, OFF_GB_LOGIT))

    def gate_chunk(c, projections):
        p_ga, p_gbl = projections
        cs = slice(c * CHUNK, (c + 1) * CHUNK)
        ga_ref[:, cs] = _bf16(_sigmoid(p_ga + bias_ref[:, cs]))
        gate_b.append(_sigmoid(p_gbl + bias_ref[:, D_MODEL + cs.start:D_MODEL + cs.stop]))

    def q_heads(tokens, q_t):
        for hd in range(N_HEADS):
            qt_ref[hd * HEAD_PAD:(hd + 1) * HEAD_PAD, tokens] = _bf16(_q_head_feature_major(
                q_t[hd * QK_HEAD_DIM:(hd + 1) * QK_HEAD_DIM, :], q_gain[:, tokens], cos_q[:, tokens], sin_q[:, tokens]))

    def k_heads(c, k_nope):
        for i in range(heads_per_chunk):
            kn = k_nope[:, i * HEAD_PAD:(i + 1) * HEAD_PAD]
            inv = lax.rsqrt((_sum_sq(kn) + k_rope_sq) * (1.0 / QK_HEAD_DIM) + NORM_EPS)
            hd = c * heads_per_chunk + i
            k_ref[hd // 2, :, (hd % 2) * HEAD_PAD:(hd % 2 + 1) * HEAD_PAD] = _bf16((kn * k_gain + k_rope_rot) * inv)

    q_kr = _dot_nt(h, wlat_t_ref[0:Q_LORA_RANK + HEAD_PAD, :])
    q_lat = q_kr[:, 0:Q_LORA_RANK]
    k_rope = q_kr[:, Q_LORA_RANK:]
    kv_lat = _dot_nt(h, wlat_t_ref[Q_LORA_RANK + HEAD_PAD:, :])
    conv_p = [conv_projections(c) for c in range(PROJ_AHEAD)]

    ang_q = invf_col_ref[...] * pos_row_ref[...]
    cos_q, sin_q = jnp.cos(ang_q), jnp.sin(ang_q)
    q_gain = qhg_ref[...] * (QK_HEAD_DIM ** -0.5 * LOG2_E)
    cos_k, sin_k = _rope_tables_token_major(cos_q, sin_q)
    k_gain = khg_ref[...]
    k_rope_g = k_rope * k_gain
    k_rope_rot = k_rope_g * cos_k + _swap_halves(k_rope_g) * sin_k
    k_rope_sq = _sum_sq(k_rope)

    qn = _bf16(_rmsnorm(q_lat, qag_ref[...], Q_LORA_RANK))
    ckv = _bf16(_rmsnorm(kv_lat, kvag_ref[...], KV_LORA_RANK))
    gate_p = []
    half = tm // 2
    for c in range(n_chunks):
        if c % (n_chunks // 2) == 0:
            wave = c // (n_chunks // 2)
            tokens = slice(wave * half, (wave + 1) * half)
            q_heads(tokens, _dot_nt(wuqt_ref[...], qn[tokens]))
        k_heads(c, _dot(ckv, wuk_ref[:, c * CHUNK:(c + 1) * CHUNK]))
        if c + PROJ_AHEAD < n_chunks:
            conv_p.append(conv_projections(c + PROJ_AHEAD))
        gate_p.append(gate_projections(c))
    v_t = _bf16(_dot_nt(wuvt_ref[...], ckv))
    pad_rows = lax.broadcasted_iota(jnp.int32, (V_TILE - V_HEAD_DIM, tm), 0)
    ones_block = jnp.where(pad_rows == 0, 1.0, 0.0).astype(jnp.bfloat16)
    for hd in range(N_HEADS):
        vt_ref[hd * V_TILE:hd * V_TILE + V_HEAD_DIM, :] = v_t[hd * V_HEAD_DIM:(hd + 1) * V_HEAD_DIM, :]
        vt_ref[hd * V_TILE + V_HEAD_DIM:(hd + 1) * V_TILE, :] = ones_block

    for c in range(n_chunks):
        conv_chunk(c, conv_p[c])
        gate_chunk(c, gate_p[c])
    y_conv = _dot(jnp.concatenate(y_b, axis=1), wpc_ref[...])
    for c, g in enumerate(gate_b):
        cs = slice(c * CHUNK, (c + 1) * CHUNK)
        gyb_ref[:, cs] = _bf16(g * y_conv[:, cs])


def _attn_kernel(qt_ref, k_ref, vt_ref, o_ref):
    n_pairs, s_len, _ = k_ref.shape
    key = lax.broadcasted_iota(jnp.int32, (TQ, TQ), 0)
    qry = lax.broadcasted_iota(jnp.int32, (TQ, TQ), 1)
    n_q = s_len // TQ
    tm = qt_ref.shape[2]

    def tokens(ref, hd, lo, hi):
        rows = ref.shape[1] // (2 * n_pairs)
        hs = slice(hd * rows, (hd + 1) * rows)
        parts = [ref[j, hs, max(lo - j * tm, 0):min(hi - j * tm, tm)] for j in range(lo // tm, -(-hi // tm))]
        return parts[0] if len(parts) == 1 else jnp.concatenate(parts, axis=1)

    def scores(qi, hd):
        lo, hi = qi * TQ, (qi + 1) * TQ
        k = k_ref[hd // 2, 0:hi, (hd % 2) * HEAD_PAD:(hd % 2 + 1) * HEAD_PAD]
        s = _dot(k, tokens(qt_ref, hd, lo, hi))
        s_diag = jnp.where(key <= qry, s[lo:hi, :], MASK_VALUE)
        s = s_diag if qi == 0 else jnp.concatenate([s[0:lo, :], s_diag], axis=0)
        return s, jnp.max(s, axis=0, keepdims=True)

    def finish(qi, hd, s_and_max):
        hi = (qi + 1) * TQ
        s, m = s_and_max
        p = jnp.exp2(s - m)
        acc = _dot(tokens(vt_ref, hd, 0, hi), _bf16(p))
        return acc[0:V_HEAD_DIM, :] / acc[V_HEAD_DIM:V_HEAD_DIM + 1, :]

    chains = [(qi, hd) for qi in reversed(range(n_q)) for hd in range(2 * n_pairs)]
    pending = [scores(*c) for c in chains[:SCORE_LOOKAHEAD]]
    outs = []
    for i, (qi, hd) in enumerate(chains):
        s = pending.pop(0)
        if i + SCORE_LOOKAHEAD < len(chains):
            pending.append(scores(*chains[i + SCORE_LOOKAHEAD]))
        outs.append(finish(qi, hd, s))
        if hd % 2 == 1:
            o_ref[hd // 2, qi * TQ:(qi + 1) * TQ, :] = _bf16(jnp.concatenate(outs, axis=0).T)
            outs = []


def _post_kernel(x_ref, attn_ref, ga_ref, gyb_ref, wpa_ref, wout_ref, gain_ref, wg_ref, wu_ref, wd_ref,
                 o_ref, act_ref):
    attn = jnp.concatenate([attn_ref[hp] for hp in range(attn_ref.shape[0])], axis=1)
    y_a = _dot(attn, wpa_ref[...])
    merged = ga_ref[...].astype(jnp.float32) * y_a + gyb_ref[...].astype(jnp.float32)
    x2 = x_ref[...] + _dot(_bf16(merged), wout_ref[...])
    o_ref[...] = _swiglu_half_step(x2, gain_ref[...], wg_ref, wu_ref, wd_ref, act_ref)


def _rows(tm, width):
    return pl.BlockSpec((tm, width), lambda i: (i, 0))


def _cols(height, tm):
    return pl.BlockSpec((height, tm), lambda i: (0, i))


def _resident():
    return pl.BlockSpec(memory_space=pltpu.VMEM)


def _params(n_axes):
    return pltpu.CompilerParams(dimension_semantics=("arbitrary",) * n_axes, vmem_limit_bytes=VMEM_LIMIT)


def _to_head_tiles(w, real):
    kdim = w.shape[0]
    w = w.reshape(kdim, -1, real)
    w = jnp.pad(w, ((0, 0), (0, 0), (0, 1)))
    src = np.where((HEAD_SRC >= 0) & (HEAD_SRC < real), HEAD_SRC, real)
    return w[:, :, src].reshape(kdim, -1)


def kernel(x, positions, ffn1_norm, ffn1_w_gate, ffn1_w_up, ffn1_w_down, mix_norm, w_in, gate_bias, q_a_norm, w_uq, kv_a_norm, w_uk, w_uv, q_head_norm, k_head_norm, w_proj_attn, conv_w, w_proj_conv, w_out, ffn2_norm, ffn2_w_gate, ffn2_w_up, ffn2_w_down):
    b, s, d = x.shape
    t = b * s
    assert d == D_MODEL and t % TM_FFN == 0 and s % TM_MIX == 0 and s % TQ == 0
    f32 = jnp.float32
    bf = jnp.bfloat16
    row = lambda g: g.reshape(1, -1).astype(f32)

    x_flat = x.reshape(t, d)
    pos_row = positions.reshape(1, t).astype(f32)

    w_in_t = _bf16(w_in).T
    w_kr_t = jnp.zeros((HEAD_PAD, D_MODEL), bf)
    w_kr_t = w_kr_t.at[ROPE_LO:HALF_TILE].set(w_in_t[LATENT_DIM:LATENT_DIM + ROPE_HALF])
    w_kr_t = w_kr_t.at[HALF_TILE + ROPE_LO:].set(w_in_t[LATENT_DIM + ROPE_HALF:REST_START])
    w_lat_t = jnp.concatenate([w_in_t[:Q_LORA_RANK], w_kr_t, w_in_t[Q_LORA_RANK:LATENT_DIM]], axis=0)
    w_rest_t = w_in_t[REST_START:]
    compact = HEAD_SRC[HEAD_SRC >= 0]
    w_uq_t = _bf16(w_uq.reshape(Q_LORA_RANK, N_HEADS, QK_HEAD_DIM)[:, :, compact].reshape(Q_LORA_RANK, -1).T)
    w_uk_pad = _bf16(_to_head_tiles(w_uk, QK_NOPE_DIM))
    w_uv_t = _bf16(w_uv.T)
    head_gain = lambda g: _to_head_tiles(g.reshape(1, QK_HEAD_DIM).astype(f32), QK_HEAD_DIM)
    q_gain_cols = jnp.broadcast_to(q_head_norm.astype(f32)[compact].reshape(QK_HEAD_DIM, 1), (QK_HEAD_DIM, TM_MIX))
    inv_freq = 1.0 / (ROPE_THETA ** (jnp.arange(ROPE_HALF, dtype=f32) / ROPE_HALF))
    inv_freq_cols = jnp.broadcast_to(inv_freq.reshape(ROPE_HALF, 1), (ROPE_HALF, TM_MIX))

    def ffn_weights(wg, wu, wd):
        return _bf16(wg), _bf16(wu), _bf16(wd)

    x1 = pl.pallas_call(
        _ffn_kernel,
        grid=(t // TM_FFN1,),
        in_specs=[_rows(TM_FFN1, d)] + [_resident()] * 4,
        out_specs=_rows(TM_FFN1, d),
        out_shape=jax.ShapeDtypeStruct((t, d), f32),
        scratch_shapes=[pltpu.VMEM((TM_FFN1, D_FF), bf)],
        compiler_params=_params(1),
        name="ffn1",
    )(x_flat, row(ffn1_norm), *ffn_weights(ffn1_w_gate, ffn1_w_up, ffn1_w_down))

    heads_w = N_HEADS * HEAD_PAD
    pair = 2 * HEAD_PAD
    n_pairs = N_HEADS // 2
    tile_major = lambda rows: pl.BlockSpec((None, rows, TM_MIX), lambda i: (i, 0, 0))
    v_rows = N_HEADS * V_TILE
    q_t, k, v_t, gate_a, gated_yb = pl.pallas_call(
        functools.partial(_mix_kernel, s),
        grid=(t // TM_MIX,),
        in_specs=[_rows(TM_MIX, d), _cols(1, TM_MIX)] + [_resident()] * 14,
        out_specs=[tile_major(heads_w), pl.BlockSpec((n_pairs, TM_MIX, pair), lambda i: (0, i, 0)), tile_major(v_rows),
                   _rows(TM_MIX, d), _rows(TM_MIX, d)],
        out_shape=[jax.ShapeDtypeStruct((t // TM_MIX, heads_w, TM_MIX), bf), jax.ShapeDtypeStruct((n_pairs, t, pair), bf),
                   jax.ShapeDtypeStruct((t // TM_MIX, v_rows, TM_MIX), bf), jax.ShapeDtypeStruct((t, d), bf),
                   jax.ShapeDtypeStruct((t, d), bf)],
        scratch_shapes=[pltpu.VMEM((SUBLANES, CONV_DIM), f32)],
        compiler_params=_params(1),
        name="mix",
    )(x1, pos_row, inv_freq_cols, row(mix_norm), w_lat_t, w_rest_t, row(gate_bias), row(q_a_norm),
      w_uq_t, row(kv_a_norm), w_uk_pad, w_uv_t, q_gain_cols, head_gain(k_head_norm),
      conv_w.astype(f32), _bf16(w_proj_conv))

    pp = ATTN_PAIRS_PER_STEP
    feature_major = lambda rows: pl.BlockSpec((s // TM_MIX, pp * rows, TM_MIX), lambda bi, hp: (bi, hp, 0))
    attn = pl.pallas_call(
        _attn_kernel,
        grid=(b, n_pairs // pp),
        in_specs=[feature_major(pair), pl.BlockSpec((pp, s, pair), lambda bi, hp: (hp, bi, 0)),
                  feature_major(2 * V_TILE)],
        out_specs=pl.BlockSpec((pp, s, LANES), lambda bi, hp: (hp, bi, 0)),
        out_shape=jax.ShapeDtypeStruct((n_pairs, t, LANES), bf),
        compiler_params=_params(2),
        name="attn",
    )(q_t, k, v_t)

    out = pl.pallas_call(
        _post_kernel,
        grid=(t // TM_FFN,),
        in_specs=[_rows(TM_FFN, d), pl.BlockSpec((n_pairs, TM_FFN, LANES), lambda i: (0, i, 0)),
                  _rows(TM_FFN, d), _rows(TM_FFN, d)]
        + [_resident()] * 6,
        out_specs=_rows(TM_FFN, d),
        out_shape=jax.ShapeDtypeStruct((t, d), f32),
        scratch_shapes=[pltpu.VMEM((TM_FFN, D_FF), bf)],
        compiler_params=_params(1),
        name="post",
    )(x1, attn, gate_a, gated_yb, _bf16(w_proj_attn), _bf16(w_out), row(ffn2_norm),
      *ffn_weights(ffn2_w_gate, ffn2_w_up, ffn2_w_down))
    return out.reshape(b, s, d)
```

```python
import functools

import jax
import jax.numpy as jnp
import numpy as np
from jax import lax
from jax.experimental import pallas as pl
from jax.experimental.pallas import tpu as pltpu

D_MODEL = 1024
D_FF = 2816
N_HEADS = 8
QK_NOPE_DIM = 64
QK_ROPE_DIM = 32
QK_HEAD_DIM = QK_NOPE_DIM + QK_ROPE_DIM
V_HEAD_DIM = 64
Q_LORA_RANK = 384
KV_LORA_RANK = 256
CONV_DIM = 1024
CONV_WIDTH = 3
ROPE_THETA = 10000.0
NORM_EPS = 1e-6

LANES = 128
SUBLANES = 8
MXU_WIDTH = 256
CHUNK = MXU_WIDTH
HEAD_PAD = LANES
BF16_ROWS = 16
V_TILE = V_HEAD_DIM + BF16_ROWS
HALF_TILE = LANES // 2
ROPE_HALF = QK_ROPE_DIM // 2
ROPE_LO = HALF_TILE - ROPE_HALF


def _head_feature_sources():
    src = np.full((HEAD_PAD,), -1, np.int64)
    src[0:ROPE_LO] = np.arange(ROPE_LO)
    src[ROPE_LO:HALF_TILE] = QK_NOPE_DIM + np.arange(ROPE_HALF)
    rest = QK_NOPE_DIM - ROPE_LO
    src[HALF_TILE:HALF_TILE + rest] = ROPE_LO + np.arange(rest)
    src[HALF_TILE + ROPE_LO:HEAD_PAD] = QK_NOPE_DIM + ROPE_HALF + np.arange(ROPE_HALF)
    return src


HEAD_SRC = _head_feature_sources()

LATENT_DIM = Q_LORA_RANK + KV_LORA_RANK
REST_START = LATENT_DIM + QK_ROPE_DIM
OFF_XC = 0
OFF_GB = OFF_XC + CONV_DIM
OFF_GC = OFF_GB + CONV_DIM
OFF_GA_LOGIT = OFF_GC + CONV_DIM
OFF_GB_LOGIT = OFF_GA_LOGIT + D_MODEL

FF_CHUNKS = (768, 768, 768, 512)
VMEM_LIMIT = 56 * 1024 * 1024

TM_FFN = 512
TM_FFN1 = 1024
TM_MIX = 512
TQ = 256
ATTN_PAIRS_PER_STEP = 4
PROJ_AHEAD = 1
SCORE_LOOKAHEAD = 3
MASK_VALUE = -1e30
LOG2_E = 1.4426950408889634


def _bf16(x):
    return x.astype(jnp.bfloat16)


def _dot(a, b):
    return jnp.dot(a, b, preferred_element_type=jnp.float32)


def _dot_nt(a, b):
    return lax.dot_general(a, b, (((1,), (1,)), ((), ())), preferred_element_type=jnp.float32)


def _sum_sq(x):
    return jnp.sum(x * x, axis=-1, keepdims=True)


def _rmsnorm(x, gain, n):
    return x * lax.rsqrt(_sum_sq(x) * (1.0 / n) + NORM_EPS) * gain


def _sigmoid(x):
    return 0.5 * jnp.tanh(0.5 * x) + 0.5


def _swiglu_half_step(x, gain, wg_ref, wu_ref, wd_ref, act_ref):
    h = _bf16(_rmsnorm(x, gain, D_MODEL))
    c0 = 0
    for ck in FF_CHUNKS:
        g = _dot(h, wg_ref[:, c0:c0 + ck])
        u = _dot(h, wu_ref[:, c0:c0 + ck])
        act_ref[:, c0:c0 + ck] = _bf16(g * _sigmoid(g) * u)
        c0 += ck
    return x + 0.5 * _dot(act_ref[...], wd_ref[...])


def _ffn_kernel(x_ref, gain_ref, wg_ref, wu_ref, wd_ref, o_ref, act_ref):
    o_ref[...] = _swiglu_half_step(x_ref[...], gain_ref[...], wg_ref, wu_ref, wd_ref, act_ref)


def _swap_halves(t):
    return pltpu.roll(t, HALF_TILE, 1)


def _rope_tables_token_major(cos, sin):
    n = cos.shape[1]
    gap = HEAD_PAD - HALF_TILE - ROPE_HALF
    ones = lambda rows: jnp.ones((rows, n), jnp.float32)
    zeros = lambda rows: jnp.zeros((rows, n), jnp.float32)
    cos_f = jnp.concatenate([ones(ROPE_LO), cos, ones(gap), cos], axis=0)
    sin_f = jnp.concatenate([zeros(ROPE_LO), -sin, zeros(gap), sin], axis=0)
    return cos_f.T, sin_f.T


def _q_head_feature_major(x, gain, cos, sin):
    inv = lax.rsqrt(jnp.sum(x * x, axis=0, keepdims=True) * (1.0 / QK_HEAD_DIM) + NORM_EPS)
    y = x * gain
    lo2 = HALF_TILE + ROPE_HALF
    y1, y2 = y[ROPE_LO:HALF_TILE], y[lo2:QK_HEAD_DIM]
    pad = jnp.zeros((HEAD_PAD - QK_HEAD_DIM, x.shape[1]), jnp.float32)
    return jnp.concatenate([y[0:ROPE_LO] * inv, (y1 * cos - y2 * sin) * inv, y[HALF_TILE:lo2] * inv, pad,
                            (y2 * cos + y1 * sin) * inv], axis=0)


def _mix_kernel(seq_len, x_ref, pos_row_ref, invf_col_ref, mixg_ref, wlat_t_ref,
                wrest_t_ref, bias_ref, qag_ref, wuqt_ref, kvag_ref, wuk_ref, wuvt_ref, qhg_ref, khg_ref,
                convw_ref, wpc_ref, qt_ref, k_ref, vt_ref, ga_ref, gyb_ref, tail_ref):
    tm = x_ref.shape[0]
    h = _bf16(_rmsnorm(x_ref[...], mixg_ref[...], D_MODEL))

    def proj(off, width):
        return _dot_nt(h, wrest_t_ref[off:off + width, :])

    @pl.when((pl.program_id(0) * tm) % seq_len == 0)
    def _():
        tail_ref[...] = jnp.zeros((SUBLANES, CONV_DIM), jnp.float32)

    head_row = lax.broadcasted_iota(jnp.int32, (SUBLANES, CHUNK), 0)

    def shifted(u, tail, shift):
        rolled = pltpu.roll(u, shift, 0)
        first = jnp.where(head_row < shift, pltpu.roll(tail, shift, 0), rolled[0:SUBLANES])
        return jnp.concatenate([first, rolled[SUBLANES:]], axis=0)

    n_chunks = CONV_DIM // CHUNK
    heads_per_chunk = N_HEADS // n_chunks
    y_b = []
    gate_b = []

    def conv_projections(c):
        return tuple(proj(off + c * CHUNK, CHUNK) for off in (OFF_GC, OFF_XC, OFF_GB))

    def conv_chunk(c, projections):
        p_gc, p_xc, p_gb = projections
        cs = slice(c * CHUNK, (c + 1) * CHUNK)
        u = p_gc * p_xc
        tail = tail_ref[:, cs]
        z = (convw_ref[0:1, cs] * shifted(u, tail, 2) + convw_ref[1:2, cs] * shifted(u, tail, 1)
             + convw_ref[2:3, cs] * u)
        tail_ref[:, cs] = u[tm - SUBLANES:tm]
        y_b.append(_bf16(p_gb * z))

    def gate_projections(c):
        return tuple(proj(off + c * CHUNK, CHUNK) for off in (OFF_GA_LOGIT, OFF_GB_LOGIT))

    def gate_chunk(c, projections):
        p_ga, p_gbl = projections
        cs = slice(c * CHUNK, (c + 1) * CHUNK)
        ga_ref[:, cs] = _bf16(_sigmoid(p_ga + bias_ref[:, cs]))
        gate_b.append(_sigmoid(p_gbl + bias_ref[:, D_MODEL + cs.start:D_MODEL + cs.stop]))

    def q_heads(tokens, q_t):
        for hd in range(N_HEADS):
            qt_ref[hd * HEAD_PAD:(hd + 1) * HEAD_PAD, tokens] = _bf16(_q_head_feature_major(
                q_t[hd * QK_HEAD_DIM:(hd + 1) * QK_HEAD_DIM, :], q_gain[:, tokens], cos_q[:, tokens], sin_q[:, tokens]))

    def k_heads(c, k_nope):
        for i in range(heads_per_chunk):
            kn = k_nope[:, i * HEAD_PAD:(i + 1) * HEAD_PAD]
            inv = lax.rsqrt((_sum_sq(kn) + k_rope_sq) * (1.0 / QK_HEAD_DIM) + NORM_EPS)
            hd = c * heads_per_chunk + i
            k_ref[hd // 2, :, (hd % 2) * HEAD_PAD:(hd % 2 + 1) * HEAD_PAD] = _bf16((kn * k_gain + k_rope_rot) * inv)

    q_kr = _dot_nt(h, wlat_t_ref[0:Q_LORA_RANK + HEAD_PAD, :])
    q_lat = q_kr[:, 0:Q_LORA_RANK]
    k_rope = q_kr[:, Q_LORA_RANK:]
    kv_lat = _dot_nt(h, wlat_t_ref[Q_LORA_RANK + HEAD_PAD:, :])
    conv_p = [conv_projections(c) for c in range(PROJ_AHEAD)]

    ang_q = invf_col_ref[...] * pos_row_ref[...]
    cos_q, sin_q = jnp.cos(ang_q), jnp.sin(ang_q)
    q_gain = qhg_ref[...] * (QK_HEAD_DIM ** -0.5 * LOG2_E)
    cos_k, sin_k = _rope_tables_token_major(cos_q, sin_q)
    k_gain = khg_ref[...]
    k_rope_g = k_rope * k_gain
    k_rope_rot = k_rope_g * cos_k + _swap_halves(k_rope_g) * sin_k
    k_rope_sq = _sum_sq(k_rope)

    qn = _bf16(_rmsnorm(q_lat, qag_ref[...], Q_LORA_RANK))
    ckv = _bf16(_rmsnorm(kv_lat, kvag_ref[...], KV_LORA_RANK))
    gate_p = []
    half = tm // 2
    for c in range(n_chunks):
        if c % (n_chunks // 2) == 0:
            wave = c // (n_chunks // 2)
            tokens = slice(wave * half, (wave + 1) * half)
            q_heads(tokens, _dot_nt(wuqt_ref[...], qn[tokens]))
        k_heads(c, _dot(ckv, wuk_ref[:, c * CHUNK:(c + 1) * CHUNK]))
        if c + PROJ_AHEAD < n_chunks:
            conv_p.append(conv_projections(c + PROJ_AHEAD))
        gate_p.append(gate_projections(c))
    v_t = _bf16(_dot_nt(wuvt_ref[...], ckv))
    pad_rows = lax.broadcasted_iota(jnp.int32, (V_TILE - V_HEAD_DIM, tm), 0)
    ones_block = jnp.where(pad_rows == 0, 1.0, 0.0).astype(jnp.bfloat16)
    for hd in range(N_HEADS):
        vt_ref[hd * V_TILE:hd * V_TILE + V_HEAD_DIM, :] = v_t[hd * V_HEAD_DIM:(hd + 1) * V_HEAD_DIM, :]
        vt_ref[hd * V_TILE + V_HEAD_DIM:(hd + 1) * V_TILE, :] = ones_block

    for c in range(n_chunks):
        conv_chunk(c, conv_p[c])
        gate_chunk(c, gate_p[c])
    y_conv = _dot(jnp.concatenate(y_b, axis=1), wpc_ref[...])
    for c, g in enumerate(gate_b):
        cs = slice(c * CHUNK, (c + 1) * CHUNK)
        gyb_ref[:, cs] = _bf16(g * y_conv[:, cs])


def _attn_kernel(qt_ref, k_ref, vt_ref, o_ref):
    n_pairs, s_len, _ = k_ref.shape
    key = lax.broadcasted_iota(jnp.int32, (TQ, TQ), 0)
    qry = lax.broadcasted_iota(jnp.int32, (TQ, TQ), 1)
    n_q = s_len // TQ
    tm = qt_ref.shape[2]

    def tokens(ref, hd, lo, hi):
        rows = ref.shape[1] // (2 * n_pairs)
        hs = slice(hd * rows, (hd + 1) * rows)
        parts = [ref[j, hs, max(lo - j * tm, 0):min(hi - j * tm, tm)] for j in range(lo // tm, -(-hi // tm))]
        return parts[0] if len(parts) == 1 else jnp.concatenate(parts, axis=1)

    def scores(qi, hd):
        lo, hi = qi * TQ, (qi + 1) * TQ
        k = k_ref[hd // 2, 0:hi, (hd % 2) * HEAD_PAD:(hd % 2 + 1) * HEAD_PAD]
        s = _dot(k, tokens(qt_ref, hd, lo, hi))
        s_diag = jnp.where(key <= qry, s[lo:hi, :], MASK_VALUE)
        s = s_diag if qi == 0 else jnp.concatenate([s[0:lo, :], s_diag], axis=0)
        return s, jnp.max(s, axis=0, keepdims=True)

    def finish(qi, hd, s_and_max):
        hi = (qi + 1) * TQ
        s, m = s_and_max
        p = jnp.exp2(s - m)
        acc = _dot(tokens(vt_ref, hd, 0, hi), _bf16(p))
        return acc[0:V_HEAD_DIM, :] / acc[V_HEAD_DIM:V_HEAD_DIM + 1, :]

    chains = [(qi, hd) for qi in reversed(range(n_q)) for hd in range(2 * n_pairs)]
    pending = [scores(*c) for c in chains[:SCORE_LOOKAHEAD]]
    outs = []
    for i, (qi, hd) in enumerate(chains):
        s = pending.pop(0)
        if i + SCORE_LOOKAHEAD < len(chains):
            pending.append(scores(*chains[i + SCORE_LOOKAHEAD]))
        outs.append(finish(qi, hd, s))
        if hd % 2 == 1:
            o_ref[hd // 2, qi * TQ:(qi + 1) * TQ, :] = _bf16(jnp.concatenate(outs, axis=0).T)
            outs = []


def _post_kernel(x_ref, attn_ref, ga_ref, gyb_ref, wpa_ref, wout_ref, gain_ref, wg_ref, wu_ref, wd_ref,
                 o_ref, act_ref):
    attn = jnp.concatenate([attn_ref[hp] for hp in range(attn_ref.shape[0])], axis=1)
    y_a = _dot(attn, wpa_ref[...])
    merged = ga_ref[...].astype(jnp.float32) * y_a + gyb_ref[...].astype(jnp.float32)
    x2 = x_ref[...] + _dot(_bf16(merged), wout_ref[...])
    o_ref[...] = _swiglu_half_step(x2, gain_ref[...], wg_ref, wu_ref, wd_ref, act_ref)


def _rows(tm, width):
    return pl.BlockSpec((tm, width), lambda i: (i, 0))


def _cols(height, tm):
    return pl.BlockSpec((height, tm), lambda i: (0, i))


def _resident():
    return pl.BlockSpec(memory_space=pltpu.VMEM)


def _params(n_axes):
    return pltpu.CompilerParams(dimension_semantics=("arbitrary",) * n_axes, vmem_limit_bytes=VMEM_LIMIT)


def _to_head_tiles(w, real):
    kdim = w.shape[0]
    w = w.reshape(kdim, -1, real)
    w = jnp.pad(w, ((0, 0), (0, 0), (0, 1)))
    src = np.where((HEAD_SRC >= 0) & (HEAD_SRC < real), HEAD_SRC, real)
    return w[:, :, src].reshape(kdim, -1)


def kernel(x, positions, ffn1_norm, ffn1_w_gate, ffn1_w_up, ffn1_w_down, mix_norm, w_in, gate_bias, q_a_norm, w_uq, kv_a_norm, w_uk, w_uv, q_head_norm, k_head_norm, w_proj_attn, conv_w, w_proj_conv, w_out, ffn2_norm, ffn2_w_gate, ffn2_w_up, ffn2_w_down):
    b, s, d = x.shape
    t = b * s
    assert d == D_MODEL and t % TM_FFN1 == 0 and s % TM_MIX == 0 and TM_MIX % TQ == 0
    assert (N_HEADS // 2) % ATTN_PAIRS_PER_STEP == 0
    f32 = jnp.float32
    bf = jnp.bfloat16
    row = lambda g: g.reshape(1, -1).astype(f32)

    x_flat = x.reshape(t, d)
    pos_row = positions.reshape(1, t).astype(f32)

    w_in_t = _bf16(w_in).T
    w_kr_t = jnp.zeros((HEAD_PAD, D_MODEL), bf)
    w_kr_t = w_kr_t.at[ROPE_LO:HALF_TILE].set(w_in_t[LATENT_DIM:LATENT_DIM + ROPE_HALF])
    w_kr_t = w_kr_t.at[HALF_TILE + ROPE_LO:].set(w_in_t[LATENT_DIM + ROPE_HALF:REST_START])
    w_lat_t = jnp.concatenate([w_in_t[:Q_LORA_RANK], w_kr_t, w_in_t[Q_LORA_RANK:LATENT_DIM]], axis=0)
    w_rest_t = w_in_t[REST_START:]
    compact = HEAD_SRC[HEAD_SRC >= 0]
    w_uq_t = _bf16(w_uq.reshape(Q_LORA_RANK, N_HEADS, QK_HEAD_DIM)[:, :, compact].reshape(Q_LORA_RANK, -1).T)
    w_uk_pad = _bf16(_to_head_tiles(w_uk, QK_NOPE_DIM))
    w_uv_t = _bf16(w_uv.T)
    head_gain = lambda g: _to_head_tiles(g.reshape(1, QK_HEAD_DIM).astype(f32), QK_HEAD_DIM)
    q_gain_cols = jnp.broadcast_to(q_head_norm.astype(f32)[compact].reshape(QK_HEAD_DIM, 1), (QK_HEAD_DIM, TM_MIX))
    inv_freq = 1.0 / (ROPE_THETA ** (jnp.arange(ROPE_HALF, dtype=f32) / ROPE_HALF))
    inv_freq_cols = jnp.broadcast_to(inv_freq.reshape(ROPE_HALF, 1), (ROPE_HALF, TM_MIX))

    def ffn_weights(wg, wu, wd):
        return _bf16(wg), _bf16(wu), _bf16(wd)

    x1 = pl.pallas_call(
        _ffn_kernel,
        grid=(t // TM_FFN1,),
        in_specs=[_rows(TM_FFN1, d)] + [_resident()] * 4,
        out_specs=_rows(TM_FFN1, d),
        out_shape=jax.ShapeDtypeStruct((t, d), f32),
        scratch_shapes=[pltpu.VMEM((TM_FFN1, D_FF), bf)],
        compiler_params=_params(1),
        name="ffn1",
    )(x_flat, row(ffn1_norm), *ffn_weights(ffn1_w_gate, ffn1_w_up, ffn1_w_down))

    heads_w = N_HEADS * HEAD_PAD
    pair = 2 * HEAD_PAD
    n_pairs = N_HEADS // 2
    tile_major = lambda rows: pl.BlockSpec((None, rows, TM_MIX), lambda i: (i, 0, 0))
    v_rows = N_HEADS * V_TILE
    q_t, k, v_t, gate_a, gated_yb = pl.pallas_call(
        functools.partial(_mix_kernel, s),
        grid=(t // TM_MIX,),
        in_specs=[_rows(TM_MIX, d), _cols(1, TM_MIX)] + [_resident()] * 14,
        out_specs=[tile_major(heads_w), pl.BlockSpec((n_pairs, TM_MIX, pair), lambda i: (0, i, 0)), tile_major(v_rows),
                   _rows(TM_MIX, d), _rows(TM_MIX, d)],
        out_shape=[jax.ShapeDtypeStruct((t // TM_MIX, heads_w, TM_MIX), bf), jax.ShapeDtypeStruct((n_pairs, t, pair), bf),
                   jax.ShapeDtypeStruct((t // TM_MIX, v_rows, TM_MIX), bf), jax.ShapeDtypeStruct((t, d), bf),
                   jax.ShapeDtypeStruct((t, d), bf)],
        scratch_shapes=[pltpu.VMEM((SUBLANES, CONV_DIM), f32)],
        compiler_params=_params(1),
        name="mix",
    )(x1, pos_row, inv_freq_cols, row(mix_norm), w_lat_t, w_rest_t, row(gate_bias), row(q_a_norm),
      w_uq_t, row(kv_a_norm), w_uk_pad, w_uv_t, q_gain_cols, head_gain(k_head_norm),
      conv_w.astype(f32), _bf16(w_proj_conv))

    pp = ATTN_PAIRS_PER_STEP
    feature_major = lambda rows: pl.BlockSpec((s // TM_MIX, pp * rows, TM_MIX), lambda bi, hp: (bi, hp, 0))
    attn = pl.pallas_call(
        _attn_kernel,
        grid=(b, n_pairs // pp),
        in_specs=[feature_major(pair), pl.BlockSpec((pp, s, pair), lambda bi, hp: (hp, bi, 0)),
                  feature_major(2 * V_TILE)],
        out_specs=pl.BlockSpec((pp, s, LANES), lambda bi, hp: (hp, bi, 0)),
        out_shape=jax.ShapeDtypeStruct((n_pairs, t, LANES), bf),
        compiler_params=_params(2),
        name="attn",
    )(q_t, k, v_t)

    out = pl.pallas_call(
        _post_kernel,
        grid=(t // TM_FFN,),
        in_specs=[_rows(TM_FFN, d), pl.BlockSpec((n_pairs, TM_FFN, LANES), lambda i: (0, i, 0)),
                  _rows(TM_FFN, d), _rows(TM_FFN, d)]
        + [_resident()] * 6,
        out_specs=_rows(TM_FFN, d),
        out_shape=jax.ShapeDtypeStruct((t, d), f32),
        scratch_shapes=[pltpu.VMEM((TM_FFN, D_FF), bf)],
        compiler_params=_params(1),
        name="post",
    )(x1, attn, gate_a, gated_yb, _bf16(w_proj_attn), _bf16(w_out), row(ffn2_norm),
      *ffn_weights(ffn2_w_gate, ffn2_w_up, ffn2_w_down))
    return out.reshape(b, s, d)
```

```python
import functools

import jax
import jax.numpy as jnp
import numpy as np
from jax import lax
from jax.experimental import pallas as pl
from jax.experimental.pallas import tpu as pltpu

D_MODEL = 1024
D_FF = 2816
N_HEADS = 8
QK_NOPE_DIM = 64
QK_ROPE_DIM = 32
QK_HEAD_DIM = QK_NOPE_DIM + QK_ROPE_DIM
V_HEAD_DIM = 64
Q_LORA_RANK = 384
KV_LORA_RANK = 256
CONV_DIM = 1024
CONV_WIDTH = 3
ROPE_THETA = 10000.0
NORM_EPS = 1e-6

LANES = 128
SUBLANES = 8
MXU_WIDTH = 256
CHUNK = MXU_WIDTH
HEAD_PAD = LANES
BF16_ROWS = 16
V_TILE = V_HEAD_DIM + BF16_ROWS
HALF_TILE = LANES // 2
ROPE_HALF = QK_ROPE_DIM // 2
ROPE_LO = HALF_TILE - ROPE_HALF


def _head_feature_sources():
    src = np.full((HEAD_PAD,), -1, np.int64)
    src[0:ROPE_LO] = np.arange(ROPE_LO)
    src[ROPE_LO:HALF_TILE] = QK_NOPE_DIM + np.arange(ROPE_HALF)
    rest = QK_NOPE_DIM - ROPE_LO
    src[HALF_TILE:HALF_TILE + rest] = ROPE_LO + np.arange(rest)
    src[HALF_TILE + ROPE_LO:HEAD_PAD] = QK_NOPE_DIM + ROPE_HALF + np.arange(ROPE_HALF)
    return src


HEAD_SRC = _head_feature_sources()

LATENT_DIM = Q_LORA_RANK + KV_LORA_RANK
REST_START = LATENT_DIM + QK_ROPE_DIM
OFF_XC = 0
OFF_GB = OFF_XC + CONV_DIM
OFF_GC = OFF_GB + CONV_DIM
OFF_GA_LOGIT = OFF_GC + CONV_DIM
OFF_GB_LOGIT = OFF_GA_LOGIT + D_MODEL

FF_CHUNKS = (768, 768, 768, 512)
VMEM_LIMIT = 56 * 1024 * 1024

TM_FFN = 512
TM_FFN1 = 1024
TM_MIX = 512
TQ = 256
PV_KEY_CHUNK = 512
ATTN_PAIRS_PER_STEP = 4
PROJ_AHEAD = 1
SCORE_LOOKAHEAD = 3
MASK_VALUE = -1e30
LOG2_E = 1.4426950408889634


def _bf16(x):
    return x.astype(jnp.bfloat16)


def _dot(a, b):
    return jnp.dot(a, b, preferred_element_type=jnp.float32)


def _dot_nt(a, b):
    return lax.dot_general(a, b, (((1,), (1,)), ((), ())), preferred_element_type=jnp.float32)


def _sum_sq(x):
    return jnp.sum(x * x, axis=-1, keepdims=True)


def _rmsnorm(x, gain, n):
    return x * lax.rsqrt(_sum_sq(x) * (1.0 / n) + NORM_EPS) * gain


def _sigmoid(x):
    return 0.5 * jnp.tanh(0.5 * x) + 0.5


def _swiglu_half_step(x, gain, wg_ref, wu_ref, wd_ref, act_ref):
    h = _bf16(_rmsnorm(x, gain, D_MODEL))
    c0 = 0
    for ck in FF_CHUNKS:
        g = _dot(h, wg_ref[:, c0:c0 + ck])
        u = _dot(h, wu_ref[:, c0:c0 + ck])
        act_ref[:, c0:c0 + ck] = _bf16(g * _sigmoid(g) * u)
        c0 += ck
    return x + 0.5 * _dot(act_ref[...], wd_ref[...])


def _ffn_kernel(x_ref, gain_ref, wg_ref, wu_ref, wd_ref, o_ref, act_ref):
    o_ref[...] = _swiglu_half_step(x_ref[...], gain_ref[...], wg_ref, wu_ref, wd_ref, act_ref)


def _swap_halves(t):
    return pltpu.roll(t, HALF_TILE, 1)


def _rope_tables_token_major(cos, sin):
    n = cos.shape[1]
    gap = HEAD_PAD - HALF_TILE - ROPE_HALF
    ones = lambda rows: jnp.ones((rows, n), jnp.float32)
    zeros = lambda rows: jnp.zeros((rows, n), jnp.float32)
    cos_f = jnp.concatenate([ones(ROPE_LO), cos, ones(gap), cos], axis=0)
    sin_f = jnp.concatenate([zeros(ROPE_LO), -sin, zeros(gap), sin], axis=0)
    return cos_f.T, sin_f.T


def _q_head_feature_major(x, gain, cos, sin):
    inv = lax.rsqrt(jnp.sum(x * x, axis=0, keepdims=True) * (1.0 / QK_HEAD_DIM) + NORM_EPS)
    y = x * gain
    lo2 = HALF_TILE + ROPE_HALF
    y1, y2 = y[ROPE_LO:HALF_TILE], y[lo2:QK_HEAD_DIM]
    pad = jnp.zeros((HEAD_PAD - QK_HEAD_DIM, x.shape[1]), jnp.float32)
    return jnp.concatenate([y[0:ROPE_LO] * inv, (y1 * cos - y2 * sin) * inv, y[HALF_TILE:lo2] * inv, pad,
                            (y2 * cos + y1 * sin) * inv], axis=0)


def _mix_kernel(seq_len, x_ref, pos_row_ref, invf_col_ref, mixg_ref, wlat_t_ref,
                wrest_t_ref, bias_ref, qag_ref, wuqt_ref, kvag_ref, wuk_ref, wuvt_ref, qhg_ref, khg_ref,
                convw_ref, wpc_ref, qt_ref, k_ref, vt_ref, ga_ref, gyb_ref, tail_ref):
    tm = x_ref.shape[0]
    h = _bf16(_rmsnorm(x_ref[...], mixg_ref[...], D_MODEL))

    def proj(off, width):
        return _dot_nt(h, wrest_t_ref[off:off + width, :])

    @pl.when((pl.program_id(0) * tm) % seq_len == 0)
    def _():
        tail_ref[...] = jnp.zeros((SUBLANES, CONV_DIM), jnp.float32)

    head_row = lax.broadcasted_iota(jnp.int32, (SUBLANES, CHUNK), 0)

    def shifted(u, tail, shift):
        rolled = pltpu.roll(u, shift, 0)
        first = jnp.where(head_row < shift, pltpu.roll(tail, shift, 0), rolled[0:SUBLANES])
        return jnp.concatenate([first, rolled[SUBLANES:]], axis=0)

    n_chunks = CONV_DIM // CHUNK
    heads_per_chunk = N_HEADS // n_chunks
    y_b = []
    gate_b = []

    def conv_projections(c):
        return tuple(proj(off + c * CHUNK, CHUNK) for off in (OFF_GC, OFF_XC, OFF_GB))

    def conv_chunk(c, projections):
        p_gc, p_xc, p_gb = projections
        cs = slice(c * CHUNK, (c + 1) * CHUNK)
        u = p_gc * p_xc
        tail = tail_ref[:, cs]
        z = (convw_ref[0:1, cs] * shifted(u, tail, 2) + convw_ref[1:2, cs] * shifted(u, tail, 1)
             + convw_ref[2:3, cs] * u)
        tail_ref[:, cs] = u[tm - SUBLANES:tm]
        y_b.append(_bf16(p_gb * z))

    def gate_projections(c):
        return tuple(proj(off + c * CHUNK, CHUNK) for off in (OFF_GA_LOGIT, OFF_GB_LOGIT))

    def gate_chunk(c, projections):
        p_ga, p_gbl = projections
        cs = slice(c * CHUNK, (c + 1) * CHUNK)
        ga_ref[:, cs] = _bf16(_sigmoid(p_ga + bias_ref[:, cs]))
        gate_b.append(_sigmoid(p_gbl + bias_ref[:, D_MODEL + cs.start:D_MODEL + cs.stop]))

    def q_heads(tokens, q_t):
        for hd in range(N_HEADS):
            qt_ref[hd * HEAD_PAD:(hd + 1) * HEAD_PAD, tokens] = _bf16(_q_head_feature_major(
                q_t[hd * QK_HEAD_DIM:(hd + 1) * QK_HEAD_DIM, :], q_gain[:, tokens], cos_q[:, tokens], sin_q[:, tokens]))

    def k_heads(c, k_nope):
        for i in range(heads_per_chunk):
            kn = k_nope[:, i * HEAD_PAD:(i + 1) * HEAD_PAD]
            inv = lax.rsqrt((_sum_sq(kn) + k_rope_sq) * (1.0 / QK_HEAD_DIM) + NORM_EPS)
            hd = c * heads_per_chunk + i
            k_ref[hd // 2, :, (hd % 2) * HEAD_PAD:(hd % 2 + 1) * HEAD_PAD] = _bf16((kn * k_gain + k_rope_rot) * inv)

    q_kr = _dot_nt(h, wlat_t_ref[0:Q_LORA_RANK + HEAD_PAD, :])
    q_lat = q_kr[:, 0:Q_LORA_RANK]
    k_rope = q_kr[:, Q_LORA_RANK:]
    kv_lat = _dot_nt(h, wlat_t_ref[Q_LORA_RANK + HEAD_PAD:, :])
    conv_p = [conv_projections(c) for c in range(PROJ_AHEAD)]

    ang_q = invf_col_ref[...] * pos_row_ref[...]
    cos_q, sin_q = jnp.cos(ang_q), jnp.sin(ang_q)
    q_gain = qhg_ref[...] * (QK_HEAD_DIM ** -0.5 * LOG2_E)
    cos_k, sin_k = _rope_tables_token_major(cos_q, sin_q)
    k_gain = khg_ref[...]
    k_rope_g = k_rope * k_gain
    k_rope_rot = k_rope_g * cos_k + _swap_halves(k_rope_g) * sin_k
    k_rope_sq = _sum_sq(k_rope)

    qn = _bf16(_rmsnorm(q_lat, qag_ref[...], Q_LORA_RANK))
    ckv = _bf16(_rmsnorm(kv_lat, kvag_ref[...], KV_LORA_RANK))
    gate_p = []
    half = tm // 2
    for c in range(n_chunks):
        if c % (n_chunks // 2) == 0:
            wave = c // (n_chunks // 2)
            tokens = slice(wave * half, (wave + 1) * half)
            q_heads(tokens, _dot_nt(wuqt_ref[...], qn[tokens]))
        k_heads(c, _dot(ckv, wuk_ref[:, c * CHUNK:(c + 1) * CHUNK]))
        if c + PROJ_AHEAD < n_chunks:
            conv_p.append(conv_projections(c + PROJ_AHEAD))
        gate_p.append(gate_projections(c))
    v_t = _bf16(_dot_nt(wuvt_ref[...], ckv))
    pad_rows = lax.broadcasted_iota(jnp.int32, (V_TILE - V_HEAD_DIM, tm), 0)
    ones_block = jnp.where(pad_rows == 0, 1.0, 0.0).astype(jnp.bfloat16)
    for hd in range(N_HEADS):
        vt_ref[hd * V_TILE:hd * V_TILE + V_HEAD_DIM, :] = v_t[hd * V_HEAD_DIM:(hd + 1) * V_HEAD_DIM, :]
        vt_ref[hd * V_TILE + V_HEAD_DIM:(hd + 1) * V_TILE, :] = ones_block

    for c in range(n_chunks):
        conv_chunk(c, conv_p[c])
        gate_chunk(c, gate_p[c])
    y_conv = _dot(jnp.concatenate(y_b, axis=1), wpc_ref[...])
    for c, g in enumerate(gate_b):
        cs = slice(c * CHUNK, (c + 1) * CHUNK)
        gyb_ref[:, cs] = _bf16(g * y_conv[:, cs])


def _attn_kernel(qt_ref, k_ref, vt_ref, o_ref):
    n_pairs, s_len, _ = k_ref.shape
    key = lax.broadcasted_iota(jnp.int32, (TQ, TQ), 0)
    qry = lax.broadcasted_iota(jnp.int32, (TQ, TQ), 1)
    n_q = s_len // TQ
    tm = qt_ref.shape[2]

    def tokens(ref, hd, lo, hi):
        rows = ref.shape[1] // (2 * n_pairs)
        hs = slice(hd * rows, (hd + 1) * rows)
        parts = [ref[j, hs, max(lo - j * tm, 0):min(hi - j * tm, tm)] for j in range(lo // tm, -(-hi // tm))]
        return parts[0] if len(parts) == 1 else jnp.concatenate(parts, axis=1)

    def scores(qi, hd):
        lo, hi = qi * TQ, (qi + 1) * TQ
        k = k_ref[hd // 2, 0:hi, (hd % 2) * HEAD_PAD:(hd % 2 + 1) * HEAD_PAD]
        s = _dot(k, tokens(qt_ref, hd, lo, hi))
        s_diag = jnp.where(key <= qry, s[lo:hi, :], MASK_VALUE)
        s = s_diag if qi == 0 else jnp.concatenate([s[0:lo, :], s_diag], axis=0)
        return s, jnp.max(s, axis=0, keepdims=True)

    def finish(qi, hd, s_and_max):
        hi = (qi + 1) * TQ
        s, m = s_and_max
        acc = None
        for k0 in range(0, hi, PV_KEY_CHUNK):
            k1 = min(k0 + PV_KEY_CHUNK, hi)
            part = _dot(tokens(vt_ref, hd, k0, k1), _bf16(jnp.exp2(s[k0:k1, :] - m)))
            acc = part if acc is None else acc + part
        return acc[0:V_HEAD_DIM, :] / acc[V_HEAD_DIM:V_HEAD_DIM + 1, :]

    chains = [(qi, hd) for qi in reversed(range(n_q)) for hd in range(2 * n_pairs)]
    pending = [scores(*c) for c in chains[:SCORE_LOOKAHEAD]]
    outs = []
    for i, (qi, hd) in enumerate(chains):
        s = pending.pop(0)
        if i + SCORE_LOOKAHEAD < len(chains):
            pending.append(scores(*chains[i + SCORE_LOOKAHEAD]))
        outs.append(finish(qi, hd, s))
        if hd % 2 == 1:
            o_ref[hd // 2, qi * TQ:(qi + 1) * TQ, :] = _bf16(jnp.concatenate(outs, axis=0).T)
            outs = []


def _post_kernel(x_ref, attn_ref, ga_ref, gyb_ref, wpa_ref, wout_ref, gain_ref, wg_ref, wu_ref, wd_ref,
                 o_ref, act_ref):
    attn = jnp.concatenate([attn_ref[hp] for hp in range(attn_ref.shape[0])], axis=1)
    y_a = _dot(attn, wpa_ref[...])
    merged = ga_ref[...].astype(jnp.float32) * y_a + gyb_ref[...].astype(jnp.float32)
    x2 = x_ref[...] + _dot(_bf16(merged), wout_ref[...])
    o_ref[...] = _swiglu_half_step(x2, gain_ref[...], wg_ref, wu_ref, wd_ref, act_ref)


def _rows(tm, width):
    return pl.BlockSpec((tm, width), lambda i: (i, 0))


def _cols(height, tm):
    return pl.BlockSpec((height, tm), lambda i: (0, i))


def _resident():
    return pl.BlockSpec(memory_space=pltpu.VMEM)


def _params(n_axes):
    return pltpu.CompilerParams(dimension_semantics=("arbitrary",) * n_axes, vmem_limit_bytes=VMEM_LIMIT)


def _to_head_tiles(w, real):
    kdim = w.shape[0]
    w = w.reshape(kdim, -1, real)
    w = jnp.pad(w, ((0, 0), (0, 0), (0, 1)))
    src = np.where((HEAD_SRC >= 0) & (HEAD_SRC < real), HEAD_SRC, real)
    return w[:, :, src].reshape(kdim, -1)


def kernel(x, positions, ffn1_norm, ffn1_w_gate, ffn1_w_up, ffn1_w_down, mix_norm, w_in, gate_bias, q_a_norm, w_uq, kv_a_norm, w_uk, w_uv, q_head_norm, k_head_norm, w_proj_attn, conv_w, w_proj_conv, w_out, ffn2_norm, ffn2_w_gate, ffn2_w_up, ffn2_w_down):
    b, s, d = x.shape
    t = b * s
    assert d == D_MODEL and t % TM_FFN1 == 0 and s % TM_MIX == 0 and TM_MIX % TQ == 0
    assert (N_HEADS // 2) % ATTN_PAIRS_PER_STEP == 0
    f32 = jnp.float32
    bf = jnp.bfloat16
    row = lambda g: g.reshape(1, -1).astype(f32)

    x_flat = x.reshape(t, d)
    pos_row = positions.reshape(1, t).astype(f32)

    w_in_t = _bf16(w_in).T
    w_kr_t = jnp.zeros((HEAD_PAD, D_MODEL), bf)
    w_kr_t = w_kr_t.at[ROPE_LO:HALF_TILE].set(w_in_t[LATENT_DIM:LATENT_DIM + ROPE_HALF])
    w_kr_t = w_kr_t.at[HALF_TILE + ROPE_LO:].set(w_in_t[LATENT_DIM + ROPE_HALF:REST_START])
    w_lat_t = jnp.concatenate([w_in_t[:Q_LORA_RANK], w_kr_t, w_in_t[Q_LORA_RANK:LATENT_DIM]], axis=0)
    w_rest_t = w_in_t[REST_START:]
    compact = HEAD_SRC[HEAD_SRC >= 0]
    w_uq_t = _bf16(w_uq.reshape(Q_LORA_RANK, N_HEADS, QK_HEAD_DIM)[:, :, compact].reshape(Q_LORA_RANK, -1).T)
    w_uk_pad = _bf16(_to_head_tiles(w_uk, QK_NOPE_DIM))
    w_uv_t = _bf16(w_uv.T)
    head_gain = lambda g: _to_head_tiles(g.reshape(1, QK_HEAD_DIM).astype(f32), QK_HEAD_DIM)
    q_gain_cols = jnp.broadcast_to(q_head_norm.astype(f32)[compact].reshape(QK_HEAD_DIM, 1), (QK_HEAD_DIM, TM_MIX))
    inv_freq = 1.0 / (ROPE_THETA ** (jnp.arange(ROPE_HALF, dtype=f32) / ROPE_HALF))
    inv_freq_cols = jnp.broadcast_to(inv_freq.reshape(ROPE_HALF, 1), (ROPE_HALF, TM_MIX))

    def ffn_weights(wg, wu, wd):
        return _bf16(wg), _bf16(wu), _bf16(wd)

    x1 = pl.pallas_call(
        _ffn_kernel,
        grid=(t // TM_FFN1,),
        in_specs=[_rows(TM_FFN1, d)] + [_resident()] * 4,
        out_specs=_rows(TM_FFN1, d),
        out_shape=jax.ShapeDtypeStruct((t, d), f32),
        scratch_shapes=[pltpu.VMEM((TM_FFN1, D_FF), bf)],
        compiler_params=_params(1),
        name="ffn1",
    )(x_flat, row(ffn1_norm), *ffn_weights(ffn1_w_gate, ffn1_w_up, ffn1_w_down))

    heads_w = N_HEADS * HEAD_PAD
    pair = 2 * HEAD_PAD
    n_pairs = N_HEADS // 2
    tile_major = lambda rows: pl.BlockSpec((None, rows, TM_MIX), lambda i: (i, 0, 0))
    v_rows = N_HEADS * V_TILE
    q_t, k, v_t, gate_a, gated_yb = pl.pallas_call(
        functools.partial(_mix_kernel, s),
        grid=(t // TM_MIX,),
        in_specs=[_rows(TM_MIX, d), _cols(1, TM_MIX)] + [_resident()] * 14,
        out_specs=[tile_major(heads_w), pl.BlockSpec((n_pairs, TM_MIX, pair), lambda i: (0, i, 0)), tile_major(v_rows),
                   _rows(TM_MIX, d), _rows(TM_MIX, d)],
        out_shape=[jax.ShapeDtypeStruct((t // TM_MIX, heads_w, TM_MIX), bf), jax.ShapeDtypeStruct((n_pairs, t, pair), bf),
                   jax.ShapeDtypeStruct((t // TM_MIX, v_rows, TM_MIX), bf), jax.ShapeDtypeStruct((t, d), bf),
                   jax.ShapeDtypeStruct((t, d), bf)],
        scratch_shapes=[pltpu.VMEM((SUBLANES, CONV_DIM), f32)],
        compiler_params=_params(1),
        name="mix",
    )(x1, pos_row, inv_freq_cols, row(mix_norm), w_lat_t, w_rest_t, row(gate_bias), row(q_a_norm),
      w_uq_t, row(kv_a_norm), w_uk_pad, w_uv_t, q_gain_cols, head_gain(k_head_norm),
      conv_w.astype(f32), _bf16(w_proj_conv))

    pp = ATTN_PAIRS_PER_STEP
    feature_major = lambda rows: pl.BlockSpec((s // TM_MIX, pp * rows, TM_MIX), lambda bi, hp: (bi, hp, 0))
    attn = pl.pallas_call(
        _attn_kernel,
        grid=(b, n_pairs // pp),
        in_specs=[feature_major(pair), pl.BlockSpec((pp, s, pair), lambda bi, hp: (hp, bi, 0)),
                  feature_major(2 * V_TILE)],
        out_specs=pl.BlockSpec((pp, s, LANES), lambda bi, hp: (hp, bi, 0)),
        out_shape=jax.ShapeDtypeStruct((n_pairs, t, LANES), bf),
        compiler_params=_params(2),
        name="attn",
    )(q_t, k, v_t)

    out = pl.pallas_call(
        _post_kernel,
        grid=(t // TM_FFN,),
        in_specs=[_rows(TM_FFN, d), pl.BlockSpec((n_pairs, TM_FFN, LANES), lambda i: (0, i, 0)),
                  _rows(TM_FFN, d), _rows(TM_FFN, d)]
        + [_resident()] * 6,
        out_specs=_rows(TM_FFN, d),
        out_shape=jax.ShapeDtypeStruct((t, d), f32),
        scratch_shapes=[pltpu.VMEM((TM_FFN, D_FF), bf)],
        compiler_params=_params(1),
        name="post",
    )(x1, attn, gate_a, gated_yb, _bf16(w_proj_attn), _bf16(w_out), row(ffn2_norm),
      *ffn_weights(ffn2_w_gate, ffn2_w_up, ffn2_w_down))
    return out.reshape(b, s, d)
```
